```python
import math
import jax, jax.numpy as jnp
from jax import lax
import numpy as np

D_MODEL = 1024
BATCH = 1
SEQ = 16384
DEPTH = 1
DEC_BATCH = 2
DEC_SEQ = 16384
PAST_LEN = 128

N_MEM = 256
MEM_HEADS = 4
MEM_DH = D_MODEL // MEM_HEADS
DIFF_HEADS = 4
DIFF_DH = 64
DIFF_VDIM = 2 * DIFF_DH
DIFF_QK_WIDTH = DIFF_HEADS * 2 * DIFF_DH
DIFF_WIDTH = DIFF_HEADS * DIFF_VDIM
FNET_GROUPS = 4
FNET_CH = 128
FNET_WIDTH = FNET_GROUPS * FNET_CH
MIX_WIDTH = DIFF_WIDTH + FNET_WIDTH
IN_WIDTH = 2 * DIFF_QK_WIDTH + DIFF_WIDTH + FNET_WIDTH
ROPE_THETA = 10000.0
Q_BLOCK = 128
N_EXPERTS = 32
TOP_K = 4
D_FF = D_MODEL
SWIGLU_LIMIT = 7.0
SWIGLU_ALPHA = 1.702
MOE_BLOCK = 128
DEEPNORM_ALPHA = (2.0 * DEPTH) ** 0.25
DEEPNORM_BETA = (8.0 * DEPTH) ** -0.25
LN_EPS = 1e-5
SUBLN_EPS = 1e-5

kernel_name = "hymba_diffattn_fnet_moe_deepnorm_encoder"


def layer_norm(x, g, b):
    xf = x.astype(jnp.float32)
    mu = jnp.mean(xf, -1, keepdims=True)
    var = jnp.mean(jnp.square(xf - mu), -1, keepdims=True)
    y = (xf - mu) * lax.rsqrt(var + LN_EPS) * g.astype(jnp.float32) + b.astype(jnp.float32)
    return y.astype(x.dtype)


def rms_norm(x, g):
    xf = x.astype(jnp.float32)
    y = xf * lax.rsqrt(jnp.mean(jnp.square(xf), -1, keepdims=True) + SUBLN_EPS) * g.astype(jnp.float32)
    return y.astype(x.dtype)


def rotary(x):
    S, dh = x.shape[1], x.shape[-1]
    half = dh // 2
    inv = 1.0 / (ROPE_THETA ** (jnp.arange(half, dtype=jnp.float32) / half))
    ang = jnp.arange(S, dtype=jnp.float32)[:, None] * inv[None, :]
    cos = jnp.concatenate([jnp.cos(ang), jnp.cos(ang)], -1)
    sin = jnp.concatenate([jnp.sin(ang), jnp.sin(ang)], -1)
    shp = (1, S) + (1,) * (x.ndim - 3) + (dh,)
    cos, sin = cos.reshape(shp), sin.reshape(shp)
    xf = x.astype(jnp.float32)
    x1, x2 = xf[..., :half], xf[..., half:]
    rot = jnp.concatenate([-x2, x1], -1)
    return (xf * cos + rot * sin).astype(x.dtype)


def diff_attention(q, k, v, lam):
    B, H, _, S, dh = q.shape
    nb = S // Q_BLOCK
    qb = q.reshape(B, H, 2, nb, Q_BLOCK, dh).transpose(3, 0, 1, 2, 4, 5)
    scale = dh ** -0.5

    def block(qi):
        s = jnp.einsum('bhmqd,bhmkd->bhmqk', qi, k).astype(jnp.float32) * scale
        p = jax.nn.softmax(s, axis=-1)
        a = p[:, :, 0] - lam * p[:, :, 1]
        return jnp.einsum('bhqk,bhkd->bhqd', a.astype(v.dtype), v)

    o = lax.map(block, qb)
    return o.transpose(1, 2, 0, 3, 4).reshape(B, H, S, v.shape[-1])


def parallel_mixer(x, w_in, lq1, lk1, lq2, lk2, subln_g, w_fnet, w_o, lambda_init):
    B, S, _ = x.shape
    h = x @ w_in
    q = h[..., :DIFF_QK_WIDTH]
    k = h[..., DIFF_QK_WIDTH:2 * DIFF_QK_WIDTH]
    v = h[..., 2 * DIFF_QK_WIDTH:2 * DIFF_QK_WIDTH + DIFF_WIDTH]
    f = h[..., 2 * DIFF_QK_WIDTH + DIFF_WIDTH:]
    q = rotary(q.reshape(B, S, DIFF_HEADS, 2, DIFF_DH)).transpose(0, 2, 3, 1, 4)
    k = rotary(k.reshape(B, S, DIFF_HEADS, 2, DIFF_DH)).transpose(0, 2, 3, 1, 4)
    v = v.reshape(B, S, DIFF_HEADS, DIFF_VDIM).transpose(0, 2, 1, 3)
    f32 = jnp.float32
    lam = (jnp.exp(jnp.sum(lq1.astype(f32) * lk1.astype(f32)))
           - jnp.exp(jnp.sum(lq2.astype(f32) * lk2.astype(f32))) + lambda_init)
    o = diff_attention(q, k, v, lam)
    o = rms_norm(o, subln_g) * (1.0 - lambda_init)
    o = o.transpose(0, 2, 1, 3).reshape(B, S, DIFF_WIDTH)
    fg = f.reshape(B, S, FNET_GROUPS, FNET_CH).astype(f32)
    fr = jnp.fft.fft2(fg, axes=(1, 3), norm='ortho').real.astype(x.dtype)
    fo = jnp.einsum('bsgc,gcd->bsgd', fr, w_fnet).reshape(B, S, FNET_WIDTH)
    return jnp.concatenate([o, fo], -1) @ w_o


def memory_attention(x, mem, w_q, w_kv, w_o):
    B, S, D = x.shape
    q = (x @ w_q).reshape(B, S, MEM_HEADS, MEM_DH)
    kv = mem @ w_kv
    k = kv[..., :D].reshape(B, N_MEM, MEM_HEADS, MEM_DH)
    v = kv[..., D:].reshape(B, N_MEM, MEM_HEADS, MEM_DH)
    s = jnp.einsum('bshd,bmhd->bhsm', q, k).astype(jnp.float32) * (MEM_DH ** -0.5)
    p = jax.nn.softmax(s, axis=-1)
    o = jnp.einsum('bhsm,bmhd->bshd', p.astype(v.dtype), v).reshape(B, S, D)
    return o @ w_o


def moe(x, w_router, b_router, w_gu, b_gu, w_down, b_down):
    B, S, D = x.shape
    T = B * S
    xf = x.reshape(T, D)
    logits = xf.astype(jnp.float32) @ w_router.astype(jnp.float32) + b_router.astype(jnp.float32)
    top_v, top_i = lax.top_k(logits, TOP_K)
    gates = jax.nn.softmax(top_v, axis=-1)
    N = T * TOP_K
    flat_e = top_i.reshape(N)
    order = jnp.argsort(flat_e, stable=True)
    sorted_e = flat_e[order]
    counts = jnp.bincount(flat_e, length=N_EXPERTS)
    padded = (counts + MOE_BLOCK - 1) // MOE_BLOCK * MOE_BLOCK
    starts = jnp.cumsum(counts) - counts
    pstarts = jnp.cumsum(padded) - padded
    dest_sorted = pstarts[sorted_e] + jnp.arange(N, dtype=jnp.int32) - starts[sorted_e]
    R = N + N_EXPERTS * MOE_BLOCK
    nb = R // MOE_BLOCK
    row_tok = jnp.zeros((R,), jnp.int32).at[dest_sorted].set((order // TOP_K).astype(jnp.int32))
    block_e = jnp.minimum(
        jnp.searchsorted(jnp.cumsum(padded), jnp.arange(nb, dtype=jnp.int32) * MOE_BLOCK, side='right'),
        N_EXPERTS - 1)

    def expert_block(args):
        tok, e = args
        hb = xf[tok] @ w_gu[e] + b_gu[e]
        g, u = hb[:, :D_FF], hb[:, D_FF:]
        g = jnp.minimum(g, SWIGLU_LIMIT)
        u = jnp.clip(u, -SWIGLU_LIMIT, SWIGLU_LIMIT)
        a = (u + 1.0) * (g * jax.nn.sigmoid(g * SWIGLU_ALPHA))
        return a @ w_down[e] + b_down[e]

    y = lax.map(expert_block, (row_tok.reshape(nb, MOE_BLOCK), block_e)).reshape(R, D)
    dest = jnp.zeros((N,), jnp.int32).at[order].set(dest_sorted.astype(jnp.int32))
    ye = y[dest].reshape(T, TOP_K, D)
    out = jnp.einsum('tk,tkd->td', gates.astype(x.dtype), ye)
    return out.reshape(B, S, D)


def setup_inputs(seed: int = 0) -> dict:
    key = jax.random.key(seed)
    ks = jax.random.split(key, 32)
    n = jax.random.normal
    L, D, E, F = DEPTH, D_MODEL, N_EXPERTS, D_FF
    f32 = jnp.float32
    return {
        "x_prompt": n(ks[0], (BATCH, SEQ, D), f32),
        "x_sample": n(ks[1], (DEC_BATCH, DEC_SEQ, D), f32),
        "mem_prompt": n(ks[2], (BATCH, N_MEM, D), f32),
        "mem_sample": n(ks[3], (DEC_BATCH, N_MEM, D), f32),
        "w_in": n(ks[4], (L, D, IN_WIDTH), f32) * D ** -0.5,
        "lambda_q1": n(ks[5], (L, DIFF_DH), f32) * 0.1,
        "lambda_k1": n(ks[6], (L, DIFF_DH), f32) * 0.1,
        "lambda_q2": n(ks[7], (L, DIFF_DH), f32) * 0.1,
        "lambda_k2": n(ks[8], (L, DIFF_DH), f32) * 0.1,
        "subln_g": 1.0 + 0.02 * n(ks[9], (L, DIFF_VDIM), f32),
        "w_fnet": n(ks[10], (L, FNET_GROUPS, FNET_CH, FNET_CH), f32) * FNET_CH ** -0.5,
        "w_o": n(ks[11], (L, MIX_WIDTH, D), f32) * MIX_WIDTH ** -0.5 * DEEPNORM_BETA,
        "ln1_g": 1.0 + 0.02 * n(ks[12], (L, D), f32),
        "ln1_b": 0.02 * n(ks[13], (L, D), f32),
        "w_mq": n(ks[14], (L, D, D), f32) * D ** -0.5,
        "w_mkv": n(ks[15], (L, D, 2 * D), f32) * D ** -0.5,
        "w_mo": n(ks[16], (L, D, D), f32) * D ** -0.5 * DEEPNORM_BETA,
        "ln2_g": 1.0 + 0.02 * n(ks[17], (L, D), f32),
        "ln2_b": 0.02 * n(ks[18], (L, D), f32),
        "w_router": n(ks[19], (L, D, E), f32) * D ** -0.5,
        "b_router": 0.01 * n(ks[20], (L, E), f32),
        "w_gu": n(ks[21], (L, E, D, 2 * F), f32) * D ** -0.5,
        "b_gu": 0.01 * n(ks[22], (L, E, 2 * F), f32),
        "w_down": n(ks[23], (L, E, F, D), f32) * F ** -0.5 * DEEPNORM_BETA,
        "b_down": 0.01 * n(ks[24], (L, E, D), f32),
        "ln3_g": 1.0 + 0.02 * n(ks[25], (L, D), f32),
        "ln3_b": 0.02 * n(ks[26], (L, D), f32),
    }


def reference(x_prompt, x_sample, mem_prompt, mem_sample, w_in, lambda_q1, lambda_k1, lambda_q2,
              lambda_k2, subln_g, w_fnet, w_o, ln1_g, ln1_b, w_mq, w_mkv, w_mo, ln2_g, ln2_b,
              w_router, b_router, w_gu, b_gu, w_down, b_down, ln3_g, ln3_b):
    def trunk(x, mem):
        for l in range(DEPTH):
            lambda_init = 0.8 - 0.6 * math.exp(-0.3 * l)
            h = parallel_mixer(x, w_in[l], lambda_q1[l], lambda_k1[l], lambda_q2[l], lambda_k2[l],
                               subln_g[l], w_fnet[l], w_o[l], lambda_init)
            x = layer_norm(DEEPNORM_ALPHA * x + h, ln1_g[l], ln1_b[l])
            h = memory_attention(x, mem, w_mq[l], w_mkv[l], w_mo[l])
            x = layer_norm(DEEPNORM_ALPHA * x + h, ln2_g[l], ln2_b[l])
            h = moe(x, w_router[l], b_router[l], w_gu[l], b_gu[l], w_down[l], b_down[l])
            x = layer_norm(DEEPNORM_ALPHA * x + h, ln3_g[l], ln3_b[l])
        return x

    y_prompt = trunk(x_prompt, mem_prompt)
    y_sample = trunk(x_sample, mem_sample)
    return (y_prompt, y_sample)
```

```python
import functools
import math

import jax
import jax.numpy as jnp
import numpy as np
from jax import lax
from jax.experimental import pallas as pl
from jax.experimental.pallas import tpu as pltpu

D_MODEL = 1024
N_MEM = 256
MEM_HEADS = 4
MEM_DH = D_MODEL // MEM_HEADS
DIFF_HEADS = 4
DIFF_DH = 64
DIFF_VDIM = 2 * DIFF_DH
QK_WIDTH = DIFF_HEADS * 2 * DIFF_DH
V_WIDTH = DIFF_HEADS * DIFF_VDIM
FNET_GROUPS = 4
FNET_CH = 128
F_WIDTH = FNET_GROUPS * FNET_CH
ROPE_THETA = 10000.0
N_EXPERTS = 32
TOP_K = 4
D_FF = D_MODEL
SWIGLU_LIMIT = 7.0
SWIGLU_ALPHA = 1.702
DEPTH = 1
DEEPNORM_ALPHA = (2.0 * DEPTH) ** 0.25
LN_EPS = 1e-5
SUBLN_EPS = 1e-5
LAMBDA_INIT = 0.8 - 0.6 * math.exp(-0.3 * 0)

LANES = 128
VMEM_LIMIT = 56 * 1024 * 1024

TS_IN = 512
TQ = 256
FN2 = 128
F1_COLS = 4096
F2_K1 = 8
TT = 256
EXPERT_BM = 256
NEG_BIG = -1e30

bf16 = jnp.bfloat16
f32 = jnp.float32


def _cparams(sem):
    return pltpu.CompilerParams(dimension_semantics=sem, vmem_limit_bytes=VMEM_LIMIT)


def _dot(a, b):
    return jnp.dot(a, b, preferred_element_type=f32)


def _dot_nt(a, b):
    return lax.dot_general(a, b, (((1,), (1,)), ((), ())), preferred_element_type=f32)


def _layer_norm(v, g, b):
    mu = jnp.mean(v, axis=-1, keepdims=True)
    var = jnp.mean(jnp.square(v - mu), axis=-1, keepdims=True)
    return (v - mu) * lax.rsqrt(var + LN_EPS) * g + b


def _inproj_kernel(x_ref, wqt_ref, wk_ref, wvt_ref, wf_ref, cosk_ref, sink_ref, cost_ref, sint_ref,
                   qz_ref, kk_ref, vt_ref, f_ref):
    xb = x_ref[0].astype(bf16)
    hk = _dot(xb, wk_ref[...])
    ck = cosk_ref[...]
    sk = sink_ref[...]
    for h in range(DIFF_HEADS):
        a = hk[:, h * LANES:(h + 1) * LANES]
        r = hk[:, QK_WIDTH + h * LANES:QK_WIDTH + (h + 1) * LANES]
        kk_ref[0, h] = (a * ck + r * sk).astype(bf16)
    hq = _dot_nt(wqt_ref[...], xb)
    ct = cost_ref[...]
    st = sint_ref[...]
    scale = DIFF_DH ** -0.5
    zeros = jnp.zeros((DIFF_DH, xb.shape[0]), bf16)
    for h in range(DIFF_HEADS):
        for m in range(2):
            c = h * 2 + m
            a = hq[c * DIFF_DH:(c + 1) * DIFF_DH]
            r = hq[QK_WIDTH + c * DIFF_DH:QK_WIDTH + (c + 1) * DIFF_DH]
            q = ((a * ct + r * st) * scale).astype(bf16)
            if m == 0:
                qz_ref[0, h, 0, :DIFF_DH, :] = q
                qz_ref[0, h, 0, DIFF_DH:, :] = zeros
            else:
                qz_ref[0, h, 1, :DIFF_DH, :] = zeros
                qz_ref[0, h, 1, DIFF_DH:, :] = q
    hv = _dot_nt(wvt_ref[...], xb)
    for h in range(DIFF_HEADS):
        vt_ref[0, h, 0] = hv[h * DIFF_VDIM:(h + 1) * DIFF_VDIM].astype(bf16)
    f_ref[0] = _dot(xb, wf_ref[...]).astype(bf16)


def _inproj(x, wqt, wk, wvt, wf, cosk, sink, cost, sint):
    B, S, D = x.shape
    ts = TS_IN
    nchunk = S // ts
    const = lambda shape: pl.BlockSpec(shape, lambda b, i: (0,) * len(shape))
    return pl.pallas_call(
        _inproj_kernel,
        grid=(B, nchunk),
        in_specs=[
            pl.BlockSpec((1, ts, D), lambda b, i: (b, i, 0)),
            const(wqt.shape), const(wk.shape), const(wvt.shape), const(wf.shape),
            pl.BlockSpec((ts, LANES), lambda b, i: (i, 0)),
            pl.BlockSpec((ts, LANES), lambda b, i: (i, 0)),
            pl.BlockSpec((DIFF_DH, ts), lambda b, i: (0, i)),
            pl.BlockSpec((DIFF_DH, ts), lambda b, i: (0, i)),
        ],
        out_specs=[
            pl.BlockSpec((1, DIFF_HEADS, 2, LANES, ts), lambda b, i: (b, 0, 0, 0, i)),
            pl.BlockSpec((1, DIFF_HEADS, ts, LANES), lambda b, i: (b, 0, i, 0)),
            pl.BlockSpec((1, DIFF_HEADS, 1, DIFF_VDIM, ts), lambda b, i: (b, 0, i, 0, 0)),
            pl.BlockSpec((1, ts, F_WIDTH), lambda b, i: (b, i, 0)),
        ],
        out_shape=[
            jax.ShapeDtypeStruct((B, DIFF_HEADS, 2, LANES, S), bf16),
            jax.ShapeDtypeStruct((B, DIFF_HEADS, S, LANES), bf16),
            jax.ShapeDtypeStruct((B, DIFF_HEADS, nchunk, DIFF_VDIM, ts), bf16),
            jax.ShapeDtypeStruct((B, S, F_WIDTH), bf16),
        ],
        compiler_params=_cparams(("parallel", "parallel")),
        name="inproj",
    )(x, wqt, wk, wvt, wf, cosk, sink, cost, sint)


def _attn_kernel(qz_ref, kk_ref, vt_ref, lam_ref, g_ref, o_ref, acc_ref):
    nchunk = vt_ref.shape[2]
    tk = vt_ref.shape[4]
    tq = qz_ref.shape[4]
    acc_ref[...] = jnp.zeros_like(acc_ref)

    def body(j, carry):
        m0, l0, m1, l1 = carry
        kt = kk_ref[0, 0, pl.ds(pl.multiple_of(j * tk, tk), tk), :]
        vt = vt_ref[0, 0, j]
        new = []
        for mp, (m_old, l_old) in enumerate(((m0, l0), (m1, l1))):
            s = _dot(kt, qz_ref[0, 0, mp])
            m_new = jnp.maximum(m_old, jnp.max(s, axis=0, keepdims=True))
            alpha = jnp.exp(m_old - m_new)
            p = jnp.exp(s - m_new)
            l_new = alpha * l_old + jnp.sum(p, axis=0, keepdims=True)
            acc_ref[mp] = acc_ref[mp] * alpha + _dot(vt, p.astype(bf16))
            new += [m_new, l_new]
        return tuple(new)

    init = (jnp.full((1, tq), NEG_BIG, f32), jnp.zeros((1, tq), f32)) * 2
    m0, l0, m1, l1 = lax.fori_loop(0, nchunk, body, init)

    lp = lam_ref[...]
    lam = (jnp.exp(jnp.sum(lp[0:1] * lp[1:2], axis=1, keepdims=True))
           - jnp.exp(jnp.sum(lp[2:3] * lp[3:4], axis=1, keepdims=True)) + LAMBDA_INIT)
    o = acc_ref[0] / l0 - lam * (acc_ref[1] / l1)
    ms = jnp.mean(jnp.square(o), axis=0, keepdims=True)
    y = o * lax.rsqrt(ms + SUBLN_EPS) * g_ref[...] * (1.0 - LAMBDA_INIT)
    o_ref[0] = y.T.astype(bf16)


def _attention(qz, kk, vt5, lam_params, g_col):
    B, H, _, _, S = qz.shape
    nchunk, tk = vt5.shape[2], vt5.shape[4]
    return pl.pallas_call(
        _attn_kernel,
        grid=(B, H, S // TQ),
        in_specs=[
            pl.BlockSpec((1, 1, 2, LANES, TQ), lambda b, h, i: (b, h, 0, 0, i)),
            pl.BlockSpec((1, 1, S, LANES), lambda b, h, i: (b, h, 0, 0)),
            pl.BlockSpec((1, 1, nchunk, DIFF_VDIM, tk), lambda b, h, i: (b, h, 0, 0, 0)),
            pl.BlockSpec(lam_params.shape, lambda b, h, i: (0, 0)),
            pl.BlockSpec(g_col.shape, lambda b, h, i: (0, 0)),
        ],
        out_specs=pl.BlockSpec((1, TQ, DIFF_VDIM), lambda b, h, i: (b, i, h)),
        out_shape=jax.ShapeDtypeStruct((B, S, V_WIDTH), bf16),
        scratch_shapes=[pltpu.VMEM((2, DIFF_VDIM, TQ), f32)],
        compiler_params=_cparams(("parallel", "parallel", "parallel")),
        name="diffattn",
    )(qz, kk, vt5, lam_params, g_col)


def _fnet1_kernel(w1_ref, f_ref, a_ref):
    a_ref[0] = _dot(w1_ref[...], f_ref[0]).astype(bf16)


def _fnet1(f2d, w1):
    B, n1, cols = f2d.shape
    return pl.pallas_call(
        _fnet1_kernel,
        grid=(B, cols // F1_COLS),
        in_specs=[pl.BlockSpec(w1.shape, lambda b, j: (0, 0)),
                  pl.BlockSpec((1, n1, F1_COLS), lambda b, j: (b, 0, j))],
        out_specs=pl.BlockSpec((1, 2 * n1, F1_COLS), lambda b, j: (b, 0, j)),
        out_shape=jax.ShapeDtypeStruct((B, 2 * n1, cols), bf16),
        compiler_params=_cparams(("parallel", "parallel")),
        name="fnet_stage1",
    )(w1, f2d)


def _fnet2_kernel(m_ref, a_ref, cc_ref, sc_ref, wf_ref, o_ref):
    n2 = a_ref.shape[3]
    for kk in range(a_ref.shape[2]):
        ari = jnp.concatenate([a_ref[0, 0, kk], a_ref[0, 1, kk]], axis=0)
        bri = _dot(m_ref[kk], ari)
        br = bri[:n2].astype(bf16)
        bi = bri[n2:].astype(bf16)
        for g in range(FNET_GROUPS):
            sl = slice(g * FNET_CH, (g + 1) * FNET_CH)
            fr = _dot(br[:, sl], cc_ref[...]) + _dot(bi[:, sl], sc_ref[...])
            fo = _dot(fr.astype(bf16), wf_ref[g])
            o_ref[0, :, kk * F_WIDTH + g * FNET_CH:kk * F_WIDTH + (g + 1) * FNET_CH] = fo.astype(bf16)


def _fnet2(a5, mtab, cc, sc, wfn):
    B, _, n1, n2, c = a5.shape
    return pl.pallas_call(
        _fnet2_kernel,
        grid=(B, n1 // F2_K1),
        in_specs=[pl.BlockSpec((F2_K1, 2 * n2, 2 * n2), lambda b, j: (j, 0, 0)),
                  pl.BlockSpec((1, 2, F2_K1, n2, c), lambda b, j: (b, 0, j, 0, 0)),
                  pl.BlockSpec(cc.shape, lambda b, j: (0, 0)),
                  pl.BlockSpec(sc.shape, lambda b, j: (0, 0)),
                  pl.BlockSpec(wfn.shape, lambda b, j: (0, 0, 0))],
        out_specs=pl.BlockSpec((1, n2, F2_K1 * c), lambda b, j: (b, 0, j)),
        out_shape=jax.ShapeDtypeStruct((B, n2, n1 * c), bf16),
        compiler_params=_cparams(("parallel", "parallel")),
        name="fnet_stage2",
    )(mtab, a5, cc, sc, wfn)


def _dft_tables(S):
    n1, n2 = S // FN2, FN2
    i1 = jnp.arange(n1, dtype=jnp.int32)
    ang1 = (2.0 * math.pi / n1) * ((i1[:, None] * i1[None, :]) % n1).astype(f32)
    w1 = jnp.concatenate([jnp.cos(ang1), -jnp.sin(ang1)], axis=0) * (n1 ** -0.5)
    k = i1[:, None, None] + n1 * jnp.arange(n2, dtype=jnp.int32)[None, :, None]
    nn = jnp.arange(n2, dtype=jnp.int32)[None, None, :]
    ang2 = (2.0 * math.pi / S) * ((k * nn) % S).astype(f32)
    c2 = jnp.cos(ang2) * (n2 ** -0.5)
    s2 = jnp.sin(ang2) * (n2 ** -0.5)
    mtab = jnp.concatenate([jnp.concatenate([c2, s2], axis=2),
                            jnp.concatenate([-s2, c2], axis=2)], axis=1)
    ic = jnp.arange(FNET_CH, dtype=jnp.int32)
    angc = (2.0 * math.pi / FNET_CH) * ((ic[:, None] * ic[None, :]) % FNET_CH).astype(f32)
    cc = jnp.cos(angc) * (FNET_CH ** -0.5)
    sc = jnp.sin(angc) * (FNET_CH ** -0.5)
    return w1.astype(bf16), mtab.astype(bf16), cc.astype(bf16), sc.astype(bf16)


def _memkv_kernel(mem_ref, w_ref, k_ref, v_ref):
    kv = _dot(mem_ref[0].astype(bf16), w_ref[...])
    k_ref[0] = kv[:, :D_MODEL].astype(bf16)
    v_ref[0] = kv[:, D_MODEL:].astype(bf16)


def _memkv(mem, w_mkv):
    B, M, D = mem.shape
    return pl.pallas_call(
        _memkv_kernel,
        grid=(B,),
        in_specs=[pl.BlockSpec((1, M, D), lambda b: (b, 0, 0)),
                  pl.BlockSpec(w_mkv.shape, lambda b: (0, 0))],
        out_specs=[pl.BlockSpec((1, M, D), lambda b: (b, 0, 0))] * 2,
        out_shape=[jax.ShapeDtypeStruct((B, M, D), bf16)] * 2,
        compiler_params=_cparams(("parallel",)),
        name="memkv",
    )(mem, w_mkv)


def _post_kernel(x_ref, o_ref, fo_ref, wo_ref, g1_ref, b1_ref, wmq_ref, km_ref, vm_ref, wmo_ref,
                 g2_ref, b2_ref, wrh_ref, wrl_ref, br_ref,
                 x2_ref, idx_ref, gate_ref, rank_ref, cnt_ref, carry_ref):
    first = jnp.logical_and(pl.program_id(0) == 0, pl.program_id(1) == 0)

    @pl.when(first)
    def _():
        carry_ref[...] = jnp.zeros_like(carry_ref)

    x = x_ref[0]
    tt = x.shape[0]
    h = _dot(o_ref[0], wo_ref[:V_WIDTH, :]) + _dot(fo_ref[0], wo_ref[V_WIDTH:, :])
    x1 = _layer_norm(DEEPNORM_ALPHA * x + h, g1_ref[...], b1_ref[...])

    qm = _dot(x1.astype(bf16), wmq_ref[...]).astype(bf16)
    heads = []
    for hh in range(MEM_HEADS):
        sl = slice(hh * MEM_DH, (hh + 1) * MEM_DH)
        s = _dot_nt(qm[:, sl], km_ref[0, :, sl]) * (MEM_DH ** -0.5)
        s = s - jnp.max(s, axis=-1, keepdims=True)
        e = jnp.exp(s)
        p = e / jnp.sum(e, axis=-1, keepdims=True)
        heads.append(_dot(p.astype(bf16), vm_ref[0, :, sl]))
    om = jnp.concatenate(heads, axis=-1).astype(bf16)
    x2 = _layer_norm(DEEPNORM_ALPHA * x1 + _dot(om, wmo_ref[...]), g2_ref[...], b2_ref[...])
    x2_ref[0] = x2

    xh = x2.astype(bf16)
    xl = (x2 - xh.astype(f32)).astype(bf16)
    logits = (_dot(xh, wrh_ref[...]) + _dot(xl, wrh_ref[...]) + _dot(xh, wrl_ref[...])) + br_ref[...]

    lane = lax.broadcasted_iota(jnp.int32, (tt, LANES), 1).astype(f32)
    work = logits
    vals, idxs = [], []
    for _ in range(TOP_K):
        mx = jnp.max(work, axis=-1, keepdims=True)
        ix = jnp.min(jnp.where(work == mx, lane, float(LANES)), axis=-1, keepdims=True)
        vals.append(mx)
        idxs.append(ix)
        work = jnp.where(lane == ix, -jnp.inf, work)
    es = [jnp.exp(v - vals[0]) for v in vals]
    den = es[0] + es[1] + es[2] + es[3]

    sel = jnp.zeros((tt, LANES), f32)
    for ix in idxs:
        sel = sel + jnp.where(lane == ix, 1.0, 0.0)
    row = lax.broadcasted_iota(jnp.int32, (tt, tt), 0)
    col = lax.broadcasted_iota(jnp.int32, (tt, tt), 1)
    ltri = jnp.where(col < row, 1.0, 0.0).astype(bf16)
    before = _dot(ltri, sel.astype(bf16)) + carry_ref[...]

    idx_out = jnp.zeros((tt, LANES), f32)
    gate_out = jnp.zeros((tt, LANES), f32)
    rank_out = jnp.zeros((tt, LANES), f32)
    for k in range(TOP_K):
        rk = jnp.sum(jnp.where(lane == idxs[k], before, 0.0), axis=-1, keepdims=True)
        idx_out = jnp.where(lane == float(k), idxs[k], idx_out)
        gate_out = jnp.where(lane == float(k), es[k] / den, gate_out)
        rank_out = jnp.where(lane == float(k), rk, rank_out)
    idx_ref[0] = idx_out.astype(jnp.int32)
    gate_ref[0] = gate_out
    rank_ref[0] = rank_out.astype(jnp.int32)
    carry_ref[...] = carry_ref[...] + jnp.sum(sel, axis=0, keepdims=True)
    cnt_ref[...] = carry_ref[...].astype(jnp.int32)


def _post(x, o, fo, wo, g1, b1, wmq, km, vm, wmo, g2, b2, wrh, wrl, br):
    B, S, D = x.shape
    tok = lambda w: pl.BlockSpec((1, TT, w), lambda b, i: (b, i, 0))
    const = lambda a: pl.BlockSpec(a.shape, lambda b, i: (0,) * a.ndim)
    mem = lambda a: pl.BlockSpec((1,) + a.shape[1:], lambda b, i: (b, 0, 0))
    return pl.pallas_call(
        _post_kernel,
        grid=(B, S // TT),
        in_specs=[tok(D), tok(V_WIDTH), tok(F_WIDTH), const(wo), const(g1), const(b1), const(wmq),
                  mem(km), mem(vm), const(wmo), const(g2), const(b2), const(wrh), const(wrl), const(br)],
        out_specs=[tok(D), tok(LANES), tok(LANES), tok(LANES),
                   pl.BlockSpec((1, LANES), lambda b, i: (0, 0))],
        out_shape=[jax.ShapeDtypeStruct((B, S, D), f32),
                   jax.ShapeDtypeStruct((B, S, LANES), jnp.int32),
                   jax.ShapeDtypeStruct((B, S, LANES), f32),
                   jax.ShapeDtypeStruct((B, S, LANES), jnp.int32),
                   jax.ShapeDtypeStruct((1, LANES), jnp.int32)],
        scratch_shapes=[pltpu.VMEM((1, LANES), f32)],
        compiler_params=_cparams(("arbitrary", "arbitrary")),
        name="post_mixer_router",
    )(x, o, fo, wo, g1, b1, wmq, km, vm, wmo, g2, b2, wrh, wrl, br)


def _row_copy_out(x_ref, xs_ref, sem, r, d):
    return pltpu.make_async_copy(x_ref.at[pl.ds(r, 1)], xs_ref.at[pl.ds(d, 1)], sem)


def _dispatch_kernel(dest_ref, x_ref, xs_in_ref, xs_ref, sem):
    del xs_in_ref
    tt = x_ref.shape[0]

    def issue(r, c):
        for k in range(TOP_K):
            _row_copy_out(x_ref, xs_ref, sem, r, dest_ref[0, 0, r * TOP_K + k]).start()
        return c

    lax.fori_loop(0, tt, issue, 0)

    def drain(r, c):
        for k in range(TOP_K):
            _row_copy_out(x_ref, xs_ref, sem, r, 0).wait()
        return c

    lax.fori_loop(0, tt, drain, 0)


def _dispatch(dest3, x2_flat, xs_zero):
    T, D = x2_flat.shape
    return pl.pallas_call(
        _dispatch_kernel,
        grid=(T // TT,),
        in_specs=[pl.BlockSpec((1, 1, TT * TOP_K), lambda i: (i, 0, 0), memory_space=pltpu.SMEM),
                  pl.BlockSpec((TT, D), lambda i: (i, 0)),
                  pl.BlockSpec(memory_space=pl.ANY)],
        out_specs=pl.BlockSpec(memory_space=pl.ANY),
        out_shape=jax.ShapeDtypeStruct(xs_zero.shape, xs_zero.dtype),
        scratch_shapes=[pltpu.SemaphoreType.DMA],
        input_output_aliases={2: 0},
        compiler_params=_cparams(("arbitrary",)),
        name="moe_dispatch",
    )(dest3, x2_flat, xs_zero)


def _expert_kernel(be_ref, xs_ref, wgu_ref, bgu_ref, wd_ref, bd_ref, ys_ref):
    del be_ref
    hb = _dot(xs_ref[...].astype(bf16), wgu_ref[0]) + bgu_ref[0]
    g = jnp.minimum(hb[:, :D_FF], SWIGLU_LIMIT)
    u = jnp.clip(hb[:, D_FF:], -SWIGLU_LIMIT, SWIGLU_LIMIT)
    a = (u + 1.0) * (g * jax.nn.sigmoid(g * SWIGLU_ALPHA))
    ys_ref[...] = _dot(a.astype(bf16), wd_ref[0]) + bd_ref[0]


def _experts(block_e, xs, wgu, bgu, wd, bd):
    R, D = xs.shape
    bm = EXPERT_BM
    grid_spec = pltpu.PrefetchScalarGridSpec(
        num_scalar_prefetch=1,
        grid=(R // bm,),
        in_specs=[pl.BlockSpec((bm, D), lambda i, be: (i, 0)),
                  pl.BlockSpec((1, D, 2 * D_FF), lambda i, be: (be[i], 0, 0)),
                  pl.BlockSpec((1, 1, 2 * D_FF), lambda i, be: (be[i], 0, 0)),
                  pl.BlockSpec((1, D_FF, D), lambda i, be: (be[i], 0, 0)),
                  pl.BlockSpec((1, 1, D), lambda i, be: (be[i], 0, 0))],
        out_specs=pl.BlockSpec((bm, D), lambda i, be: (i, 0)),
    )
    return pl.pallas_call(
        _expert_kernel,
        grid_spec=grid_spec,
        out_shape=jax.ShapeDtypeStruct((R, D), f32),
        compiler_params=_cparams(("arbitrary",)),
        name="moe_experts",
    )(block_e, xs, wgu, bgu, wd, bd)


def _row_copy_in(ys_ref, buf_ref, sem, k, r, d):
    return pltpu.make_async_copy(ys_ref.at[pl.ds(d, 1)], buf_ref.at[k, pl.ds(r, 1)], sem)


def _combine_kernel(dest_ref, ys_ref, x2_ref, gate_ref, g3_ref, b3_ref, out_ref, buf_ref, sem):
    tt = x2_ref.shape[0]

    def issue(r, c):
        for k in range(TOP_K):
            _row_copy_in(ys_ref, buf_ref, sem, k, r, dest_ref[0, 0, r * TOP_K + k]).start()
        return c

    lax.fori_loop(0, tt, issue, 0)

    def drain(r, c):
        for k in range(TOP_K):
            _row_copy_in(ys_ref, buf_ref, sem, k, r, 0).wait()
        return c

    lax.fori_loop(0, tt, drain, 0)

    gates = gate_ref[...]
    y = gates[:, 0:1] * buf_ref[0]
    for k in range(1, TOP_K):
        y = y + gates[:, k:k + 1] * buf_ref[k]
    out_ref[...] = _layer_norm(DEEPNORM_ALPHA * x2_ref[...] + y, g3_ref[...], b3_ref[...])


def _combine(dest3, ys, x2_flat, gates_flat, g3, b3):
    T, D = x2_flat.shape
    return pl.pallas_call(
        _combine_kernel,
        grid=(T // TT,),
        in_specs=[pl.BlockSpec((1, 1, TT * TOP_K), lambda i: (i, 0, 0), memory_space=pltpu.SMEM),
                  pl.BlockSpec(memory_space=pl.ANY),
                  pl.BlockSpec((TT, D), lambda i: (i, 0)),
                  pl.BlockSpec((TT, LANES), lambda i: (i, 0)),
                  pl.BlockSpec(g3.shape, lambda i: (0, 0)),
                  pl.BlockSpec(b3.shape, lambda i: (0, 0))],
        out_specs=pl.BlockSpec((TT, D), lambda i: (i, 0)),
        out_shape=jax.ShapeDtypeStruct((T, D), f32),
        scratch_shapes=[pltpu.VMEM((TOP_K, TT, D), f32), pltpu.SemaphoreType.DMA],
        compiler_params=_cparams(("arbitrary",)),
        name="moe_combine",
    )(dest3, ys, x2_flat, gates_flat, g3, b3)


def _rot_cols(w):
    d, n = w.shape
    w4 = w.reshape(d, n // DIFF_DH, 2, DIFF_DH // 2)
    return jnp.concatenate([-w4[:, :, 1], w4[:, :, 0]], axis=-1).reshape(d, n)


def _rope_tables(S):
    half = DIFF_DH // 2
    inv = 1.0 / (ROPE_THETA ** (jnp.arange(half, dtype=f32) / half))
    ang = jnp.arange(S, dtype=f32)[:, None] * inv[None, :]
    cos = jnp.concatenate([jnp.cos(ang), jnp.cos(ang)], -1)
    sin = jnp.concatenate([jnp.sin(ang), jnp.sin(ang)], -1)
    return (jnp.concatenate([cos, cos], -1), jnp.concatenate([sin, sin], -1), cos.T, sin.T)


def _prep_weights(w_in, w_fnet, w_o, w_mq, w_mkv, w_mo, w_router, b_router, w_gu, w_down):
    wq = w_in[:, :QK_WIDTH]
    wk = w_in[:, QK_WIDTH:2 * QK_WIDTH]
    wv = w_in[:, 2 * QK_WIDTH:2 * QK_WIDTH + V_WIDTH]
    wf = w_in[:, 2 * QK_WIDTH + V_WIDTH:]
    wqt = jnp.concatenate([wq, _rot_cols(wq)], axis=1).T.astype(bf16)
    wkk = jnp.concatenate([wk, _rot_cols(wk)], axis=1).astype(bf16)
    wvt = wv.T.astype(bf16)
    pad = LANES - N_EXPERTS
    wr = jnp.pad(w_router, ((0, 0), (0, pad)))
    wrh = wr.astype(bf16)
    wrl = (wr - wrh.astype(f32)).astype(bf16)
    br = jnp.pad(b_router, (0, pad), constant_values=NEG_BIG).reshape(1, LANES)
    return dict(wqt=wqt, wk=wkk, wvt=wvt, wf=wf.astype(bf16), wfn=w_fnet.astype(bf16),
                wo=w_o.astype(bf16), wmq=w_mq.astype(bf16), wmkv=w_mkv.astype(bf16),
                wmo=w_mo.astype(bf16), wrh=wrh, wrl=wrl, br=br,
                wgu=w_gu.astype(bf16), wd=w_down.astype(bf16))


def _trunk(x, mem, W, lam_params, g_col, ln, b_gu, b_down):
    B, S, D = x.shape
    cosk, sink, cost, sint = _rope_tables(S)
    qz, kk, vt5, f = _inproj(x, W["wqt"], W["wk"], W["wvt"], W["wf"], cosk, sink, cost, sint)
    o = _attention(qz, kk, vt5, lam_params, g_col)

    n1 = S // FN2
    w1, mtab, cc, sc = _dft_tables(S)
    a = _fnet1(f.reshape(B, n1, FN2 * F_WIDTH), w1)
    fo = _fnet2(a.reshape(B, 2, n1, FN2, F_WIDTH), mtab, cc, sc, W["wfn"]).reshape(B, S, F_WIDTH)

    km, vm = _memkv(mem, W["wmkv"])
    x2, idx, gates, rank, counts = _post(x, o, fo, W["wo"], ln["g1"], ln["b1"], W["wmq"], km, vm,
                                         W["wmo"], ln["g2"], ln["b2"], W["wrh"], W["wrl"], W["br"])

    T = B * S
    N = T * TOP_K
    bm = EXPERT_BM
    counts = counts[0, :N_EXPERTS]
    padded = (counts + bm - 1) // bm * bm
    pstarts = jnp.cumsum(padded) - padded
    R = N + N_EXPERTS * bm
    nb = R // bm
    block_e = jnp.minimum(
        jnp.searchsorted(jnp.cumsum(padded), jnp.arange(nb, dtype=jnp.int32) * bm, side="right"),
        N_EXPERTS - 1).astype(jnp.int32)
    top_i = idx.reshape(T, LANES)[:, :TOP_K]
    onehot = top_i[:, :, None] == jnp.arange(N_EXPERTS, dtype=jnp.int32)[None, None, :]
    dest = jnp.sum(jnp.where(onehot, pstarts[None, None, :], 0), axis=-1) + rank.reshape(T, LANES)[:, :TOP_K]
    dest3 = dest.astype(jnp.int32).reshape(T // TT, 1, TT * TOP_K)

    x2f = x2.reshape(T, D)
    xs = _dispatch(dest3, x2f, jnp.zeros((R, D), f32))
    ys = _experts(block_e, xs, W["wgu"], b_gu, W["wd"], b_down)
    out = _combine(dest3, ys, x2f, gates.reshape(T, LANES), ln["g3"], ln["b3"])
    return out.reshape(B, S, D)


def kernel(x_prompt, x_sample, mem_prompt, mem_sample, w_in, lambda_q1, lambda_k1, lambda_q2, lambda_k2,
           subln_g, w_fnet, w_o, ln1_g, ln1_b, w_mq, w_mkv, w_mo, ln2_g, ln2_b, w_router, b_router,
           w_gu, b_gu, w_down, b_down, ln3_g, ln3_b):
    l = 0
    W = _prep_weights(w_in[l], w_fnet[l], w_o[l], w_mq[l], w_mkv[l], w_mo[l], w_router[l], b_router[l],
                      w_gu[l], w_down[l])
    lam_params = jnp.stack([lambda_q1[l], lambda_k1[l], lambda_q2[l], lambda_k2[l]], axis=0)
    g_col = subln_g[l].reshape(DIFF_VDIM, 1)
    row = lambda v: v[l].reshape(1, -1)
    ln = dict(g1=row(ln1_g), b1=row(ln1_b), g2=row(ln2_g), b2=row(ln2_b), g3=row(ln3_g), b3=row(ln3_b))
    bgu = b_gu[l].reshape(N_EXPERTS, 1, 2 * D_FF)
    bdn = b_down[l].reshape(N_EXPERTS, 1, D_MODEL)
    y_prompt = _trunk(x_prompt, mem_prompt, W, lam_params, g_col, ln, bgu, bdn)
    y_sample = _trunk(x_sample, mem_sample, W, lam_params, g_col, ln, bgu, bdn)
    return (y_prompt, y_sample)
```

```python
import functools
import math

import jax
import jax.numpy as jnp
import numpy as np
from jax import lax
from jax.experimental import pallas as pl
from jax.experimental.pallas import tpu as pltpu

D_MODEL = 1024
N_MEM = 256
MEM_HEADS = 4
MEM_DH = D_MODEL // MEM_HEADS
DIFF_HEADS = 4
DIFF_DH = 64
DIFF_VDIM = 2 * DIFF_DH
QK_WIDTH = DIFF_HEADS * 2 * DIFF_DH
V_WIDTH = DIFF_HEADS * DIFF_VDIM
FNET_GROUPS = 4
FNET_CH = 128
F_WIDTH = FNET_GROUPS * FNET_CH
ROPE_THETA = 10000.0
N_EXPERTS = 32
TOP_K = 4
D_FF = D_MODEL
SWIGLU_LIMIT = 7.0
SWIGLU_ALPHA = 1.702
DEPTH = 1
DEEPNORM_ALPHA = (2.0 * DEPTH) ** 0.25
LN_EPS = 1e-5
SUBLN_EPS = 1e-5
LAMBDA_INIT = 0.8 - 0.6 * math.exp(-0.3 * 0)
LOG2_E = math.log2(math.e)
BF16_SUBLANES = 16
V_ROWS = DIFF_VDIM + BF16_SUBLANES

LANES = 128
VMEM_LIMIT = 56 * 1024 * 1024

TS_IN = 512
TQ = 256
ATTN_PV_LAG = 3
ATTN_UNROLL = 12
FN2 = 128
F1_COLS = 4096
F2_K1 = 8
TT = 256
EXPERT_BM = 256
NEG_BIG = -1e30

bf16 = jnp.bfloat16
f32 = jnp.float32


def _cparams(sem):
    return pltpu.CompilerParams(dimension_semantics=sem, vmem_limit_bytes=VMEM_LIMIT)


def _dot(a, b):
    return jnp.dot(a, b, preferred_element_type=f32)


def _dot_nt(a, b):
    return lax.dot_general(a, b, (((1,), (1,)), ((), ())), preferred_element_type=f32)


def _layer_norm(v, g, b):
    mu = jnp.mean(v, axis=-1, keepdims=True)
    var = jnp.mean(jnp.square(v - mu), axis=-1, keepdims=True)
    return (v - mu) * lax.rsqrt(var + LN_EPS) * g + b


def _inproj_kernel(x_ref, wqt_ref, wk_ref, wvt_ref, wf_ref, cosk_ref, sink_ref, cost_ref, sint_ref,
                   qz_ref, kk_ref, vt_ref, f_ref):
    xb = x_ref[0].astype(bf16)
    hk = _dot(xb, wk_ref[...])
    ck = cosk_ref[...]
    sk = sink_ref[...]
    for h in range(DIFF_HEADS):
        a = hk[:, h * LANES:(h + 1) * LANES]
        r = hk[:, QK_WIDTH + h * LANES:QK_WIDTH + (h + 1) * LANES]
        kk_ref[0, h] = (a * ck + r * sk).astype(bf16)
    hq = _dot_nt(wqt_ref[...], xb)
    ct = cost_ref[...]
    st = sint_ref[...]
    scale = DIFF_DH ** -0.5 * LOG2_E
    zeros = jnp.zeros((DIFF_DH, xb.shape[0]), bf16)
    for h in range(DIFF_HEADS):
        for m in range(2):
            c = h * 2 + m
            a = hq[c * DIFF_DH:(c + 1) * DIFF_DH]
            r = hq[QK_WIDTH + c * DIFF_DH:QK_WIDTH + (c + 1) * DIFF_DH]
            q = ((a * ct + r * st) * scale).astype(bf16)
            if m == 0:
                qz_ref[0, h, 0, :DIFF_DH, :] = q
                qz_ref[0, h, 0, DIFF_DH:, :] = zeros
            else:
                qz_ref[0, h, 1, :DIFF_DH, :] = zeros
                qz_ref[0, h, 1, DIFF_DH:, :] = q
    hv = _dot_nt(wvt_ref[...], xb)
    sub = lax.broadcasted_iota(jnp.int32, (V_ROWS - DIFF_VDIM, xb.shape[0]), 0)
    ones_rows = jnp.where(sub == 0, 1.0, 0.0).astype(bf16)
    for h in range(DIFF_HEADS):
        vt_ref[0, h, 0, :DIFF_VDIM, :] = hv[h * DIFF_VDIM:(h + 1) * DIFF_VDIM].astype(bf16)
        vt_ref[0, h, 0, DIFF_VDIM:, :] = ones_rows
    f_ref[0] = _dot(xb, wf_ref[...]).astype(bf16)


def _inproj(x, wqt, wk, wvt, wf, cosk, sink, cost, sint):
    B, S, D = x.shape
    ts = TS_IN
    nchunk = S // ts
    const = lambda shape: pl.BlockSpec(shape, lambda b, i: (0,) * len(shape))
    return pl.pallas_call(
        _inproj_kernel,
        grid=(B, nchunk),
        in_specs=[
            pl.BlockSpec((1, ts, D), lambda b, i: (b, i, 0)),
            const(wqt.shape), const(wk.shape), const(wvt.shape), const(wf.shape),
            pl.BlockSpec((ts, LANES), lambda b, i: (i, 0)),
            pl.BlockSpec((ts, LANES), lambda b, i: (i, 0)),
            pl.BlockSpec((DIFF_DH, ts), lambda b, i: (0, i)),
            pl.BlockSpec((DIFF_DH, ts), lambda b, i: (0, i)),
        ],
        out_specs=[
            pl.BlockSpec((1, DIFF_HEADS, 2, LANES, ts), lambda b, i: (b, 0, 0, 0, i)),
            pl.BlockSpec((1, DIFF_HEADS, ts, LANES), lambda b, i: (b, 0, i, 0)),
            pl.BlockSpec((1, DIFF_HEADS, 1, V_ROWS, ts), lambda b, i: (b, 0, i, 0, 0)),
            pl.BlockSpec((1, ts, F_WIDTH), lambda b, i: (b, i, 0)),
        ],
        out_shape=[
            jax.ShapeDtypeStruct((B, DIFF_HEADS, 2, LANES, S), bf16),
            jax.ShapeDtypeStruct((B, DIFF_HEADS, S, LANES), bf16),
            jax.ShapeDtypeStruct((B, DIFF_HEADS, nchunk, V_ROWS, ts), bf16),
            jax.ShapeDtypeStruct((B, S, F_WIDTH), bf16),
        ],
        compiler_params=_cparams(("parallel", "parallel")),
        name="inproj",
    )(x, wqt, wk, wvt, wf, cosk, sink, cost, sint)


def _attn_kernel(qz_ref, kk_ref, vt_ref, lam_ref, g_ref, o_ref, s_scr, p_scr, acc_ref):
    nchunk = vt_ref.shape[2]
    tk = vt_ref.shape[4]
    tq = qz_ref.shape[4]
    acc_ref[...] = jnp.zeros_like(acc_ref)

    def scores(j, slot):
        kt = kk_ref[0, 0, pl.ds(pl.multiple_of(j * tk, tk), tk), :]
        cms = []
        for mp in range(2):
            s = _dot(kt, qz_ref[0, 0, mp])
            s_scr[slot, mp] = s
            cms.append(jnp.max(s, axis=0, keepdims=True))
        return tuple(cms)

    def probs(s_slot, p_slot, cms, ms):
        m_new, alphas = [], []
        for mp in range(2):
            m = jnp.maximum(ms[mp], cms[mp])
            alphas.append(jnp.exp2(ms[mp] - m))
            p_scr[p_slot, mp] = jnp.exp2(s_scr[s_slot, mp] - m).astype(bf16)
            m_new.append(m)
        return tuple(m_new), tuple(alphas)

    def accumulate(j, slot, alphas):
        vt = vt_ref[0, 0, j]
        for mp in range(2):
            acc_ref[mp] = acc_ref[mp] * alphas[mp] + _dot(vt, p_scr[slot, mp])

    lag = p_scr.shape[0]
    unroll = ATTN_UNROLL
    assert lag >= 2 and unroll % 2 == 0 and unroll % lag == 0 and nchunk >= lag

    def tick(t, r, ms, cm_prev, als):
        peeled = isinstance(t, int)
        cm = cm_prev
        if not peeled or t < nchunk:
            cm = scores(t, r % 2)
        if not peeled or 1 <= t <= nchunk:
            ms, al_new = probs((r - 1) % 2, (r - 1) % lag, cm_prev, ms)
            als = als + (al_new,)
        if not peeled or t >= lag:
            accumulate(t - lag, (r - lag) % lag, als[0])
            als = als[1:]
        return ms, cm, als

    first = lag + (nchunk - lag) % unroll
    ms = (jnp.full((1, tq), NEG_BIG, f32),) * 2
    cm, als = None, ()
    for t in range(first):
        ms, cm, als = tick(t, t % unroll, ms, cm, als)

    def ticks(it, carry):
        ms, cm, als = carry
        t0 = first + it * unroll
        for u in range(unroll):
            ms, cm, als = tick(t0 + u, (first + u) % unroll, ms, cm, als)
        return ms, cm, als

    ms, cm, als = lax.fori_loop(0, (nchunk - first) // unroll, ticks, (ms, cm, als))
    for t in range(nchunk, nchunk + lag):
        ms, cm, als = tick(t, t % unroll, ms, cm, als)

    lp = lam_ref[...]
    lam = (jnp.exp(jnp.sum(lp[0:1] * lp[1:2], axis=1, keepdims=True))
           - jnp.exp(jnp.sum(lp[2:3] * lp[3:4], axis=1, keepdims=True)) + LAMBDA_INIT)
    a0 = acc_ref[0]
    a1 = acc_ref[1]
    o = (a0[:DIFF_VDIM] / a0[DIFF_VDIM:DIFF_VDIM + 1]
         - lam * (a1[:DIFF_VDIM] / a1[DIFF_VDIM:DIFF_VDIM + 1]))
    ms2 = jnp.mean(jnp.square(o), axis=0, keepdims=True)
    y = o * lax.rsqrt(ms2 + SUBLN_EPS) * g_ref[...] * (1.0 - LAMBDA_INIT)
    o_ref[0] = y.T.astype(bf16)


def _attention(qz, kk, vt5, lam_params, g_col):
    B, H, _, _, S = qz.shape
    nchunk, tk = vt5.shape[2], vt5.shape[4]
    return pl.pallas_call(
        _attn_kernel,
        grid=(B, H, S // TQ),
        in_specs=[
            pl.BlockSpec((1, 1, 2, LANES, TQ), lambda b, h, i: (b, h, 0, 0, i)),
            pl.BlockSpec((1, 1, S, LANES), lambda b, h, i: (b, h, 0, 0)),
            pl.BlockSpec((1, 1, nchunk, V_ROWS, tk), lambda b, h, i: (b, h, 0, 0, 0)),
            pl.BlockSpec(lam_params.shape, lambda b, h, i: (0, 0)),
            pl.BlockSpec(g_col.shape, lambda b, h, i: (0, 0)),
        ],
        out_specs=pl.BlockSpec((1, TQ, DIFF_VDIM), lambda b, h, i: (b, i, h)),
        out_shape=jax.ShapeDtypeStruct((B, S, V_WIDTH), bf16),
        scratch_shapes=[pltpu.VMEM((2, 2, tk, TQ), f32),
                        pltpu.VMEM((ATTN_PV_LAG, 2, tk, TQ), bf16),
                        pltpu.VMEM((2, V_ROWS, TQ), f32)],
        compiler_params=_cparams(("parallel", "parallel", "parallel")),
        name="diffattn",
    )(qz, kk, vt5, lam_params, g_col)


def _fnet1_kernel(w1_ref, f_ref, a_ref):
    a_ref[0] = _dot(w1_ref[...], f_ref[0]).astype(bf16)


def _fnet1(f2d, w1):
    B, n1, cols = f2d.shape
    return pl.pallas_call(
        _fnet1_kernel,
        grid=(B, cols // F1_COLS),
        in_specs=[pl.BlockSpec(w1.shape, lambda b, j: (0, 0)),
                  pl.BlockSpec((1, n1, F1_COLS), lambda b, j: (b, 0, j))],
        out_specs=pl.BlockSpec((1, 2 * n1, F1_COLS), lambda b, j: (b, 0, j)),
        out_shape=jax.ShapeDtypeStruct((B, 2 * n1, cols), bf16),
        compiler_params=_cparams(("parallel", "parallel")),
        name="fnet_stage1",
    )(w1, f2d)


def _fnet2_kernel(m_ref, a_ref, cc_ref, sc_ref, wf_ref, o_ref):
    n2 = a_ref.shape[3]
    for kk in range(a_ref.shape[2]):
        ari = jnp.concatenate([a_ref[0, 0, kk], a_ref[0, 1, kk]], axis=0)
        bri = _dot(m_ref[kk], ari)
        br = bri[:n2].astype(bf16)
        bi = bri[n2:].astype(bf16)
        for g in range(FNET_GROUPS):
            sl = slice(g * FNET_CH, (g + 1) * FNET_CH)
            fr = _dot(br[:, sl], cc_ref[...]) + _dot(bi[:, sl], sc_ref[...])
            fo = _dot(fr.astype(bf16), wf_ref[g])
            o_ref[0, :, kk * F_WIDTH + g * FNET_CH:kk * F_WIDTH + (g + 1) * FNET_CH] = fo.astype(bf16)


def _fnet2(a5, mtab, cc, sc, wfn):
    B, _, n1, n2, c = a5.shape
    return pl.pallas_call(
        _fnet2_kernel,
        grid=(B, n1 // F2_K1),
        in_specs=[pl.BlockSpec((F2_K1, 2 * n2, 2 * n2), lambda b, j: (j, 0, 0)),
                  pl.BlockSpec((1, 2, F2_K1, n2, c), lambda b, j: (b, 0, j, 0, 0)),
                  pl.BlockSpec(cc.shape, lambda b, j: (0, 0)),
                  pl.BlockSpec(sc.shape, lambda b, j: (0, 0)),
                  pl.BlockSpec(wfn.shape, lambda b, j: (0, 0, 0))],
        out_specs=pl.BlockSpec((1, n2, F2_K1 * c), lambda b, j: (b, 0, j)),
        out_shape=jax.ShapeDtypeStruct((B, n2, n1 * c), bf16),
        compiler_params=_cparams(("parallel", "parallel")),
        name="fnet_stage2",
    )(mtab, a5, cc, sc, wfn)


def _dft_tables(S):
    n1, n2 = S // FN2, FN2
    i1 = jnp.arange(n1, dtype=jnp.int32)
    ang1 = (2.0 * math.pi / n1) * ((i1[:, None] * i1[None, :]) % n1).astype(f32)
    w1 = jnp.concatenate([jnp.cos(ang1), -jnp.sin(ang1)], axis=0) * (n1 ** -0.5)
    k = i1[:, None, None] + n1 * jnp.arange(n2, dtype=jnp.int32)[None, :, None]
    nn = jnp.arange(n2, dtype=jnp.int32)[None, None, :]
    ang2 = (2.0 * math.pi / S) * ((k * nn) % S).astype(f32)
    c2 = jnp.cos(ang2) * (n2 ** -0.5)
    s2 = jnp.sin(ang2) * (n2 ** -0.5)
    mtab = jnp.concatenate([jnp.concatenate([c2, s2], axis=2),
                            jnp.concatenate([-s2, c2], axis=2)], axis=1)
    ic = jnp.arange(FNET_CH, dtype=jnp.int32)
    angc = (2.0 * math.pi / FNET_CH) * ((ic[:, None] * ic[None, :]) % FNET_CH).astype(f32)
    cc = jnp.cos(angc) * (FNET_CH ** -0.5)
    sc = jnp.sin(angc) * (FNET_CH ** -0.5)
    return w1.astype(bf16), mtab.astype(bf16), cc.astype(bf16), sc.astype(bf16)


def _memkv_kernel(mem_ref, w_ref, k_ref, v_ref):
    kv = _dot(mem_ref[0].astype(bf16), w_ref[...])
    k_ref[0] = kv[:, :D_MODEL].astype(bf16)
    v_ref[0] = kv[:, D_MODEL:].astype(bf16)


def _memkv(mem, w_mkv):
    B, M, D = mem.shape
    return pl.pallas_call(
        _memkv_kernel,
        grid=(B,),
        in_specs=[pl.BlockSpec((1, M, D), lambda b: (b, 0, 0)),
                  pl.BlockSpec(w_mkv.shape, lambda b: (0, 0))],
        out_specs=[pl.BlockSpec((1, M, D), lambda b: (b, 0, 0))] * 2,
        out_shape=[jax.ShapeDtypeStruct((B, M, D), bf16)] * 2,
        compiler_params=_cparams(("parallel",)),
        name="memkv",
    )(mem, w_mkv)


def _post_kernel(x_ref, o_ref, fo_ref, wo_ref, g1_ref, b1_ref, wmq_ref, km_ref, vm_ref, wmo_ref,
                 g2_ref, b2_ref, wrh_ref, wrl_ref, br_ref,
                 x2_ref, idx_ref, gate_ref, rank_ref, cnt_ref, carry_ref):
    first = jnp.logical_and(pl.program_id(0) == 0, pl.program_id(1) == 0)

    @pl.when(first)
    def _():
        carry_ref[...] = jnp.zeros_like(carry_ref)

    x = x_ref[0]
    tt = x.shape[0]
    h = _dot(o_ref[0], wo_ref[:V_WIDTH, :]) + _dot(fo_ref[0], wo_ref[V_WIDTH:, :])
    x1 = _layer_norm(DEEPNORM_ALPHA * x + h, g1_ref[...], b1_ref[...])

    qm = _dot(x1.astype(bf16), wmq_ref[...]).astype(bf16)
    heads = []
    for hh in range(MEM_HEADS):
        sl = slice(hh * MEM_DH, (hh + 1) * MEM_DH)
        s = _dot_nt(qm[:, sl], km_ref[0, :, sl]) * (MEM_DH ** -0.5)
        s = s - jnp.max(s, axis=-1, keepdims=True)
        e = jnp.exp(s)
        p = e / jnp.sum(e, axis=-1, keepdims=True)
        heads.append(_dot(p.astype(bf16), vm_ref[0, :, sl]))
    om = jnp.concatenate(heads, axis=-1).astype(bf16)
    x2 = _layer_norm(DEEPNORM_ALPHA * x1 + _dot(om, wmo_ref[...]), g2_ref[...], b2_ref[...])
    x2_ref[0] = x2

    xh = x2.astype(bf16)
    xl = (x2 - xh.astype(f32)).astype(bf16)
    logits = (_dot(xh, wrh_ref[...]) + _dot(xl, wrh_ref[...]) + _dot(xh, wrl_ref[...])) + br_ref[...]

    lane = lax.broadcasted_iota(jnp.int32, (tt, LANES), 1).astype(f32)
    work = logits
    vals, idxs = [], []
    for _ in range(TOP_K):
        mx = jnp.max(work, axis=-1, keepdims=True)
        ix = jnp.min(jnp.where(work == mx, lane, float(LANES)), axis=-1, keepdims=True)
        vals.append(mx)
        idxs.append(ix)
        work = jnp.where(lane == ix, -jnp.inf, work)
    es = [jnp.exp(v - vals[0]) for v in vals]
    den = es[0] + es[1] + es[2] + es[3]

    sel = jnp.zeros((tt, LANES), f32)
    for ix in idxs:
        sel = sel + jnp.where(lane == ix, 1.0, 0.0)
    row = lax.broadcasted_iota(jnp.int32, (tt, tt), 0)
    col = lax.broadcasted_iota(jnp.int32, (tt, tt), 1)
    ltri = jnp.where(col < row, 1.0, 0.0).astype(bf16)
    before = _dot(ltri, sel.astype(bf16)) + carry_ref[...]

    idx_out = jnp.zeros((tt, LANES), f32)
    gate_out = jnp.zeros((tt, LANES), f32)
    rank_out = jnp.zeros((tt, LANES), f32)
    for k in range(TOP_K):
        rk = jnp.sum(jnp.where(lane == idxs[k], before, 0.0), axis=-1, keepdims=True)
        idx_out = jnp.where(lane == float(k), idxs[k], idx_out)
        gate_out = jnp.where(lane == float(k), es[k] / den, gate_out)
        rank_out = jnp.where(lane == float(k), rk, rank_out)
    idx_ref[0] = idx_out.astype(jnp.int32)
    gate_ref[0] = gate_out
    rank_ref[0] = rank_out.astype(jnp.int32)
    carry_ref[...] = carry_ref[...] + jnp.sum(sel, axis=0, keepdims=True)
    cnt_ref[...] = carry_ref[...].astype(jnp.int32)


def _post(x, o, fo, wo, g1, b1, wmq, km, vm, wmo, g2, b2, wrh, wrl, br):
    B, S, D = x.shape
    tok = lambda w: pl.BlockSpec((1, TT, w), lambda b, i: (b, i, 0))
    const = lambda a: pl.BlockSpec(a.shape, lambda b, i: (0,) * a.ndim)
    mem = lambda a: pl.BlockSpec((1,) + a.shape[1:], lambda b, i: (b, 0, 0))
    return pl.pallas_call(
        _post_kernel,
        grid=(B, S // TT),
        in_specs=[tok(D), tok(V_WIDTH), tok(F_WIDTH), const(wo), const(g1), const(b1), const(wmq),
                  mem(km), mem(vm), const(wmo), const(g2), const(b2), const(wrh), const(wrl), const(br)],
        out_specs=[tok(D), tok(LANES), tok(LANES), tok(LANES),
                   pl.BlockSpec((1, LANES), lambda b, i: (0, 0))],
        out_shape=[jax.ShapeDtypeStruct((B, S, D), f32),
                   jax.ShapeDtypeStruct((B, S, LANES), jnp.int32),
                   jax.ShapeDtypeStruct((B, S, LANES), f32),
                   jax.ShapeDtypeStruct((B, S, LANES), jnp.int32),
                   jax.ShapeDtypeStruct((1, LANES), jnp.int32)],
        scratch_shapes=[pltpu.VMEM((1, LANES), f32)],
        compiler_params=_cparams(("arbitrary", "arbitrary")),
        name="post_mixer_router",
    )(x, o, fo, wo, g1, b1, wmq, km, vm, wmo, g2, b2, wrh, wrl, br)


def _row_copy_out(x_ref, xs_ref, sem, r, d):
    return pltpu.make_async_copy(x_ref.at[pl.ds(r, 1)], xs_ref.at[pl.ds(d, 1)], sem)


def _dispatch_kernel(dest_ref, x_ref, xs_in_ref, xs_ref, sem):
    del xs_in_ref
    tt = x_ref.shape[0]

    def issue(r, c):
        for k in range(TOP_K):
            _row_copy_out(x_ref, xs_ref, sem, r, dest_ref[0, 0, r * TOP_K + k]).start()
        return c

    lax.fori_loop(0, tt, issue, 0)

    def drain(r, c):
        for k in range(TOP_K):
            _row_copy_out(x_ref, xs_ref, sem, r, 0).wait()
        return c

    lax.fori_loop(0, tt, drain, 0)


def _dispatch(dest3, x2_flat, xs_zero):
    T, D = x2_flat.shape
    return pl.pallas_call(
        _dispatch_kernel,
        grid=(T // TT,),
        in_specs=[pl.BlockSpec((1, 1, TT * TOP_K), lambda i: (i, 0, 0), memory_space=pltpu.SMEM),
                  pl.BlockSpec((TT, D), lambda i: (i, 0)),
                  pl.BlockSpec(memory_space=pl.ANY)],
        out_specs=pl.BlockSpec(memory_space=pl.ANY),
        out_shape=jax.ShapeDtypeStruct(xs_zero.shape, xs_zero.dtype),
        scratch_shapes=[pltpu.SemaphoreType.DMA],
        input_output_aliases={2: 0},
        compiler_params=_cparams(("arbitrary",)),
        name="moe_dispatch",
    )(dest3, x2_flat, xs_zero)


def _expert_kernel(be_ref, xs_ref, wgu_ref, bgu_ref, wd_ref, bd_ref, ys_ref):
    del be_ref
    hb = _dot(xs_ref[...].astype(bf16), wgu_ref[0]) + bgu_ref[0]
    g = jnp.minimum(hb[:, :D_FF], SWIGLU_LIMIT)
    u = jnp.clip(hb[:, D_FF:], -SWIGLU_LIMIT, SWIGLU_LIMIT)
    a = (u + 1.0) * (g * jax.nn.sigmoid(g * SWIGLU_ALPHA))
    ys_ref[...] = _dot(a.astype(bf16), wd_ref[0]) + bd_ref[0]


def _experts(block_e, xs, wgu, bgu, wd, bd):
    R, D = xs.shape
    bm = EXPERT_BM
    grid_spec = pltpu.PrefetchScalarGridSpec(
        num_scalar_prefetch=1,
        grid=(R // bm,),
        in_specs=[pl.BlockSpec((bm, D), lambda i, be: (i, 0)),
                  pl.BlockSpec((1, D, 2 * D_FF), lambda i, be: (be[i], 0, 0)),
                  pl.BlockSpec((1, 1, 2 * D_FF), lambda i, be: (be[i], 0, 0)),
                  pl.BlockSpec((1, D_FF, D), lambda i, be: (be[i], 0, 0)),
                  pl.BlockSpec((1, 1, D), lambda i, be: (be[i], 0, 0))],
        out_specs=pl.BlockSpec((bm, D), lambda i, be: (i, 0)),
    )
    return pl.pallas_call(
        _expert_kernel,
        grid_spec=grid_spec,
        out_shape=jax.ShapeDtypeStruct((R, D), f32),
        compiler_params=_cparams(("arbitrary",)),
        name="moe_experts",
    )(block_e, xs, wgu, bgu, wd, bd)


def _row_copy_in(ys_ref, buf_ref, sem, k, r, d):
    return pltpu.make_async_copy(ys_ref.at[pl.ds(d, 1)], buf_ref.at[k, pl.ds(r, 1)], sem)


def _combine_kernel(dest_ref, ys_ref, x2_ref, gate_ref, g3_ref, b3_ref, out_ref, buf_ref, sem):
    tt = x2_ref.shape[0]

    def issue(r, c):
        for k in range(TOP_K):
            _row_copy_in(ys_ref, buf_ref, sem, k, r, dest_ref[0, 0, r * TOP_K + k]).start()
        return c

    lax.fori_loop(0, tt, issue, 0)

    def drain(r, c):
        for k in range(TOP_K):
            _row_copy_in(ys_ref, buf_ref, sem, k, r, 0).wait()
        return c

    lax.fori_loop(0, tt, drain, 0)

    gates = gate_ref[...]
    y = gates[:, 0:1] * buf_ref[0]
    for k in range(1, TOP_K):
        y = y + gates[:, k:k + 1] * buf_ref[k]
    out_ref[...] = _layer_norm(DEEPNORM_ALPHA * x2_ref[...] + y, g3_ref[...], b3_ref[...])


def _combine(dest3, ys, x2_flat, gates_flat, g3, b3):
    T, D = x2_flat.shape
    return pl.pallas_call(
        _combine_kernel,
        grid=(T // TT,),
        in_specs=[pl.BlockSpec((1, 1, TT * TOP_K), lambda i: (i, 0, 0), memory_space=pltpu.SMEM),
                  pl.BlockSpec(memory_space=pl.ANY),
                  pl.BlockSpec((TT, D), lambda i: (i, 0)),
                  pl.BlockSpec((TT, LANES), lambda i: (i, 0)),
                  pl.BlockSpec(g3.shape, lambda i: (0, 0)),
                  pl.BlockSpec(b3.shape, lambda i: (0, 0))],
        out_specs=pl.BlockSpec((TT, D), lambda i: (i, 0)),
        out_shape=jax.ShapeDtypeStruct((T, D), f32),
        scratch_shapes=[pltpu.VMEM((TOP_K, TT, D), f32), pltpu.SemaphoreType.DMA],
        compiler_params=_cparams(("arbitrary",)),
        name="moe_combine",
    )(dest3, ys, x2_flat, gates_flat, g3, b3)


def _rot_cols(w):
    d, n = w.shape
    w4 = w.reshape(d, n // DIFF_DH, 2, DIFF_DH // 2)
    return jnp.concatenate([-w4[:, :, 1], w4[:, :, 0]], axis=-1).reshape(d, n)


def _rope_tables(S):
    half = DIFF_DH // 2
    inv = 1.0 / (ROPE_THETA ** (jnp.arange(half, dtype=f32) / half))
    ang = jnp.arange(S, dtype=f32)[:, None] * inv[None, :]
    cos = jnp.concatenate([jnp.cos(ang), jnp.cos(ang)], -1)
    sin = jnp.concatenate([jnp.sin(ang), jnp.sin(ang)], -1)
    return (jnp.concatenate([cos, cos], -1), jnp.concatenate([sin, sin], -1), cos.T, sin.T)


def _prep_weights(w_in, w_fnet, w_o, w_mq, w_mkv, w_mo, w_router, b_router, w_gu, w_down):
    wq = w_in[:, :QK_WIDTH]
    wk = w_in[:, QK_WIDTH:2 * QK_WIDTH]
    wv = w_in[:, 2 * QK_WIDTH:2 * QK_WIDTH + V_WIDTH]
    wf = w_in[:, 2 * QK_WIDTH + V_WIDTH:]
    wqt = jnp.concatenate([wq, _rot_cols(wq)], axis=1).T.astype(bf16)
    wkk = jnp.concatenate([wk, _rot_cols(wk)], axis=1).astype(bf16)
    wvt = wv.T.astype(bf16)
    pad = LANES - N_EXPERTS
    wr = jnp.pad(w_router, ((0, 0), (0, pad)))
    wrh = wr.astype(bf16)
    wrl = (wr - wrh.astype(f32)).astype(bf16)
    br = jnp.pad(b_router, (0, pad), constant_values=NEG_BIG).reshape(1, LANES)
    return dict(wqt=wqt, wk=wkk, wvt=wvt, wf=wf.astype(bf16), wfn=w_fnet.astype(bf16),
                wo=w_o.astype(bf16), wmq=w_mq.astype(bf16), wmkv=w_mkv.astype(bf16),
                wmo=w_mo.astype(bf16), wrh=wrh, wrl=wrl, br=br,
                wgu=w_gu.astype(bf16), wd=w_down.astype(bf16))


def _trunk(x, mem, W, tables, lam_params, g_col, ln, b_gu, b_down):
    B, S, D = x.shape
    cosk, sink, cost, sint = tables["rope"]
    qz, kk, vt5, f = _inproj(x, W["wqt"], W["wk"], W["wvt"], W["wf"], cosk, sink, cost, sint)
    o = _attention(qz, kk, vt5, lam_params, g_col)

    n1 = S // FN2
    w1, mtab, cc, sc = tables["dft"]
    a = _fnet1(f.reshape(B, n1, FN2 * F_WIDTH), w1)
    fo = _fnet2(a.reshape(B, 2, n1, FN2, F_WIDTH), mtab, cc, sc, W["wfn"]).reshape(B, S, F_WIDTH)

    km, vm = _memkv(mem, W["wmkv"])
    x2, idx, gates, rank, counts = _post(x, o, fo, W["wo"], ln["g1"], ln["b1"], W["wmq"], km, vm,
                                         W["wmo"], ln["g2"], ln["b2"], W["wrh"], W["wrl"], W["br"])

    T = B * S
    N = T * TOP_K
    bm = EXPERT_BM
    counts = counts[0, :N_EXPERTS]
    padded = (counts + bm - 1) // bm * bm
    pstarts = jnp.cumsum(padded) - padded
    R = N + N_EXPERTS * bm
    nb = R // bm
    ends = jnp.cumsum(padded)
    block_e = jnp.minimum(
        jnp.sum((ends[None, :] <= (jnp.arange(nb, dtype=jnp.int32) * bm)[:, None]).astype(jnp.int32), axis=1),
        N_EXPERTS - 1).astype(jnp.int32)
    top_i = idx.reshape(T, LANES)[:, :TOP_K]
    onehot = top_i[:, :, None] == jnp.arange(N_EXPERTS, dtype=jnp.int32)[None, None, :]
    dest = jnp.sum(jnp.where(onehot, pstarts[None, None, :], 0), axis=-1) + rank.reshape(T, LANES)[:, :TOP_K]
    dest3 = dest.astype(jnp.int32).reshape(T // TT, 1, TT * TOP_K)

    x2f = x2.reshape(T, D)
    xs = _dispatch(dest3, x2f, jnp.zeros((R, D), f32))
    ys = _experts(block_e, xs, W["wgu"], b_gu, W["wd"], b_down)
    out = _combine(dest3, ys, x2f, gates.reshape(T, LANES), ln["g3"], ln["b3"])
    return out.reshape(B, S, D)


def kernel(x_prompt, x_sample, mem_prompt, mem_sample, w_in, lambda_q1, lambda_k1, lambda_q2, lambda_k2,
           subln_g, w_fnet, w_o, ln1_g, ln1_b, w_mq, w_mkv, w_mo, ln2_g, ln2_b, w_router, b_router,
           w_gu, b_gu, w_down, b_down, ln3_g, ln3_b):
    l = 0
    W = _prep_weights(w_in[l], w_fnet[l], w_o[l], w_mq[l], w_mkv[l], w_mo[l], w_router[l], b_router[l],
                      w_gu[l], w_down[l])
    lam_params = jnp.stack([lambda_q1[l], lambda_k1[l], lambda_q2[l], lambda_k2[l]], axis=0)
    g_col = subln_g[l].reshape(DIFF_VDIM, 1)
    row = lambda v: v[l].reshape(1, -1)
    ln = dict(g1=row(ln1_g), b1=row(ln1_b), g2=row(ln2_g), b2=row(ln2_b), g3=row(ln3_g), b3=row(ln3_b))
    bgu = b_gu[l].reshape(N_EXPERTS, 1, 2 * D_FF)
    bdn = b_down[l].reshape(N_EXPERTS, 1, D_MODEL)
    tables = {}
    outs = []
    for x, mem in ((x_prompt, mem_prompt), (x_sample, mem_sample)):
        S = x.shape[1]
        if S not in tables:
            tables[S] = dict(rope=_rope_tables(S), dft=_dft_tables(S))
        outs.append(_trunk(x, mem, W, tables[S], lam_params, g_col, ln, bgu, bdn))
    return tuple(outs)
```

```python
import functools
import math

import jax
import jax.numpy as jnp
import numpy as np
from jax import lax
from jax.experimental import pallas as pl
from jax.experimental.pallas import tpu as pltpu

D_MODEL = 1024
N_MEM = 256
MEM_HEADS = 4
MEM_DH = D_MODEL // MEM_HEADS
DIFF_HEADS = 4
DIFF_DH = 64
DIFF_VDIM = 2 * DIFF_DH
QK_WIDTH = DIFF_HEADS * 2 * DIFF_DH
V_WIDTH = DIFF_HEADS * DIFF_VDIM
FNET_GROUPS = 4
FNET_CH = 128
F_WIDTH = FNET_GROUPS * FNET_CH
ROPE_THETA = 10000.0
N_EXPERTS = 32
TOP_K = 4
D_FF = D_MODEL
SWIGLU_LIMIT = 7.0
SWIGLU_ALPHA = 1.702
DEPTH = 1
DEEPNORM_ALPHA = (2.0 * DEPTH) ** 0.25
LN_EPS = 1e-5
SUBLN_EPS = 1e-5
LAMBDA_INIT = 0.8 - 0.6 * math.exp(-0.3 * 0)
LOG2_E = math.log2(math.e)
BF16_SUBLANES = 16
V_ROWS = DIFF_VDIM + BF16_SUBLANES

LANES = 128
SUBLANES = 8
VMEM_LIMIT = 56 * 1024 * 1024

TS_IN = 512
TQ = 256
ATTN_PV_LAG = 3
ATTN_UNROLL = 12
FN2 = 128
F1_COLS = 4096
F2_K1 = 8
POST_TT = 1024
POST_SUB = 256
TT = 256
EXPERT_BM = 256
DMA_UNROLL = 4
NEG_BIG = -1e30

bf16 = jnp.bfloat16
f32 = jnp.float32


def _cparams(sem):
    return pltpu.CompilerParams(dimension_semantics=sem, vmem_limit_bytes=VMEM_LIMIT)


def _dot(a, b):
    return jnp.dot(a, b, preferred_element_type=f32)


def _dot_nt(a, b):
    return lax.dot_general(a, b, (((1,), (1,)), ((), ())), preferred_element_type=f32)


def _load_token_tiles(ref, lead, n):
    return jnp.concatenate([ref[lead + (pl.ds(j, n, stride=SUBLANES),)] for j in range(D_MODEL // LANES)], axis=-1)


def _store_token_tiles(ref, lead, row0, v):
    for j in range(D_MODEL // LANES):
        ref[lead + (pl.ds(row0 * SUBLANES + j, v.shape[0], stride=SUBLANES),)] = v[:, j * LANES:(j + 1) * LANES]


def _layer_norm(v, g, b):
    mu = jnp.mean(v, axis=-1, keepdims=True)
    var = jnp.mean(jnp.square(v - mu), axis=-1, keepdims=True)
    return (v - mu) * lax.rsqrt(var + LN_EPS) * g + b


def _inproj_kernel(x_ref, wqt_ref, wk_ref, wvt_ref, wf_ref, cosk_ref, sink_ref, cost_ref, sint_ref,
                   qz_ref, kk_ref, vt_ref, f_ref):
    xb = x_ref[0].astype(bf16)
    hk = _dot(xb, wk_ref[...])
    ck = cosk_ref[...]
    sk = sink_ref[...]
    for h in range(DIFF_HEADS):
        a = hk[:, h * LANES:(h + 1) * LANES]
        r = hk[:, QK_WIDTH + h * LANES:QK_WIDTH + (h + 1) * LANES]
        kk_ref[0, h] = (a * ck + r * sk).astype(bf16)
    hq = _dot_nt(wqt_ref[...], xb)
    ct = cost_ref[...]
    st = sint_ref[...]
    scale = DIFF_DH ** -0.5 * LOG2_E
    zeros = jnp.zeros((DIFF_DH, xb.shape[0]), bf16)
    for h in range(DIFF_HEADS):
        for m in range(2):
            c = h * 2 + m
            a = hq[c * DIFF_DH:(c + 1) * DIFF_DH]
            r = hq[QK_WIDTH + c * DIFF_DH:QK_WIDTH + (c + 1) * DIFF_DH]
            q = ((a * ct + r * st) * scale).astype(bf16)
            if m == 0:
                qz_ref[0, h, 0, :DIFF_DH, :] = q
                qz_ref[0, h, 0, DIFF_DH:, :] = zeros
            else:
                qz_ref[0, h, 1, :DIFF_DH, :] = zeros
                qz_ref[0, h, 1, DIFF_DH:, :] = q
    hv = _dot_nt(wvt_ref[...], xb)
    sub = lax.broadcasted_iota(jnp.int32, (V_ROWS - DIFF_VDIM, xb.shape[0]), 0)
    ones_rows = jnp.where(sub == 0, 1.0, 0.0).astype(bf16)
    for h in range(DIFF_HEADS):
        vt_ref[0, h, 0, :DIFF_VDIM, :] = hv[h * DIFF_VDIM:(h + 1) * DIFF_VDIM].astype(bf16)
        vt_ref[0, h, 0, DIFF_VDIM:, :] = ones_rows
    f_ref[0] = _dot(xb, wf_ref[...]).astype(bf16)


def _inproj(x, wqt, wk, wvt, wf, cosk, sink, cost, sint):
    B, S, D = x.shape
    ts = TS_IN
    nchunk = S // ts
    const = lambda shape: pl.BlockSpec(shape, lambda b, i: (0,) * len(shape))
    return pl.pallas_call(
        _inproj_kernel,
        grid=(B, nchunk),
        in_specs=[
            pl.BlockSpec((1, ts, D), lambda b, i: (b, i, 0)),
            const(wqt.shape), const(wk.shape), const(wvt.shape), const(wf.shape),
            pl.BlockSpec((ts, LANES), lambda b, i: (i, 0)),
            pl.BlockSpec((ts, LANES), lambda b, i: (i, 0)),
            pl.BlockSpec((DIFF_DH, ts), lambda b, i: (0, i)),
            pl.BlockSpec((DIFF_DH, ts), lambda b, i: (0, i)),
        ],
        out_specs=[
            pl.BlockSpec((1, DIFF_HEADS, 2, LANES, ts), lambda b, i: (b, 0, 0, 0, i)),
            pl.BlockSpec((1, DIFF_HEADS, ts, LANES), lambda b, i: (b, 0, i, 0)),
            pl.BlockSpec((1, DIFF_HEADS, 1, V_ROWS, ts), lambda b, i: (b, 0, i, 0, 0)),
            pl.BlockSpec((1, ts, F_WIDTH), lambda b, i: (b, i, 0)),
        ],
        out_shape=[
            jax.ShapeDtypeStruct((B, DIFF_HEADS, 2, LANES, S), bf16),
            jax.ShapeDtypeStruct((B, DIFF_HEADS, S, LANES), bf16),
            jax.ShapeDtypeStruct((B, DIFF_HEADS, nchunk, V_ROWS, ts), bf16),
            jax.ShapeDtypeStruct((B, S, F_WIDTH), bf16),
        ],
        compiler_params=_cparams(("parallel", "parallel")),
        name="inproj",
    )(x, wqt, wk, wvt, wf, cosk, sink, cost, sint)


def _attn_kernel(qz_ref, kk_ref, vt_ref, lam_ref, g_ref, o_ref, s_scr, p_scr, acc_ref):
    nchunk = vt_ref.shape[2]
    tk = vt_ref.shape[4]
    tq = qz_ref.shape[4]
    acc_ref[...] = jnp.zeros_like(acc_ref)

    def scores(j, slot):
        kt = kk_ref[0, 0, pl.ds(pl.multiple_of(j * tk, tk), tk), :]
        cms = []
        for mp in range(2):
            s = _dot(kt, qz_ref[0, 0, mp])
            s_scr[slot, mp] = s
            cms.append(jnp.max(s, axis=0, keepdims=True))
        return tuple(cms)

    def probs(s_slot, p_slot, cms, ms):
        m_new, alphas = [], []
        for mp in range(2):
            m = jnp.maximum(ms[mp], cms[mp])
            alphas.append(jnp.exp2(ms[mp] - m))
            p_scr[p_slot, mp] = jnp.exp2(s_scr[s_slot, mp] - m).astype(bf16)
            m_new.append(m)
        return tuple(m_new), tuple(alphas)

    def accumulate(j, slot, alphas):
        vt = vt_ref[0, 0, j]
        for mp in range(2):
            acc_ref[mp] = acc_ref[mp] * alphas[mp] + _dot(vt, p_scr[slot, mp])

    lag = p_scr.shape[0]
    unroll = ATTN_UNROLL
    assert lag >= 2 and unroll % 2 == 0 and unroll % lag == 0 and nchunk >= lag

    def tick(t, r, ms, cm_prev, als):
        peeled = isinstance(t, int)
        cm = cm_prev
        if not peeled or t < nchunk:
            cm = scores(t, r % 2)
        if not peeled or 1 <= t <= nchunk:
            ms, al_new = probs((r - 1) % 2, (r - 1) % lag, cm_prev, ms)
            als = als + (al_new,)
        if not peeled or t >= lag:
            accumulate(t - lag, (r - lag) % lag, als[0])
            als = als[1:]
        return ms, cm, als

    first = lag + (nchunk - lag) % unroll
    ms = (jnp.full((1, tq), NEG_BIG, f32),) * 2
    cm, als = None, ()
    for t in range(first):
        ms, cm, als = tick(t, t % unroll, ms, cm, als)

    def ticks(it, carry):
        ms, cm, als = carry
        t0 = first + it * unroll
        for u in range(unroll):
            ms, cm, als = tick(t0 + u, (first + u) % unroll, ms, cm, als)
        return ms, cm, als

    ms, cm, als = lax.fori_loop(0, (nchunk - first) // unroll, ticks, (ms, cm, als))
    for t in range(nchunk, nchunk + lag):
        ms, cm, als = tick(t, t % unroll, ms, cm, als)

    lp = lam_ref[...]
    lam = (jnp.exp(jnp.sum(lp[0:1] * lp[1:2], axis=1, keepdims=True))
           - jnp.exp(jnp.sum(lp[2:3] * lp[3:4], axis=1, keepdims=True)) + LAMBDA_INIT)
    a0 = acc_ref[0]
    a1 = acc_ref[1]
    o = (a0[:DIFF_VDIM] / a0[DIFF_VDIM:DIFF_VDIM + 1]
         - lam * (a1[:DIFF_VDIM] / a1[DIFF_VDIM:DIFF_VDIM + 1]))
    ms2 = jnp.mean(jnp.square(o), axis=0, keepdims=True)
    y = o * lax.rsqrt(ms2 + SUBLN_EPS) * g_ref[...] * (1.0 - LAMBDA_INIT)
    o_ref[0] = y.T.astype(bf16)


def _attention(qz, kk, vt5, lam_params, g_col):
    B, H, _, _, S = qz.shape
    nchunk, tk = vt5.shape[2], vt5.shape[4]
    return pl.pallas_call(
        _attn_kernel,
        grid=(B, H, S // TQ),
        in_specs=[
            pl.BlockSpec((1, 1, 2, LANES, TQ), lambda b, h, i: (b, h, 0, 0, i)),
            pl.BlockSpec((1, 1, S, LANES), lambda b, h, i: (b, h, 0, 0)),
            pl.BlockSpec((1, 1, nchunk, V_ROWS, tk), lambda b, h, i: (b, h, 0, 0, 0)),
            pl.BlockSpec(lam_params.shape, lambda b, h, i: (0, 0)),
            pl.BlockSpec(g_col.shape, lambda b, h, i: (0, 0)),
        ],
        out_specs=pl.BlockSpec((1, TQ, DIFF_VDIM), lambda b, h, i: (b, i, h)),
        out_shape=jax.ShapeDtypeStruct((B, S, V_WIDTH), bf16),
        scratch_shapes=[pltpu.VMEM((2, 2, tk, TQ), f32),
                        pltpu.VMEM((ATTN_PV_LAG, 2, tk, TQ), bf16),
                        pltpu.VMEM((2, V_ROWS, TQ), f32)],
        compiler_params=_cparams(("parallel", "parallel", "parallel")),
        name="diffattn",
    )(qz, kk, vt5, lam_params, g_col)


def _fnet1_kernel(w1_ref, f_ref, a_ref):
    a_ref[0] = _dot(w1_ref[...], f_ref[0]).astype(bf16)


def _fnet1(f2d, w1):
    B, n1, cols = f2d.shape
    return pl.pallas_call(
        _fnet1_kernel,
        grid=(B, cols // F1_COLS),
        in_specs=[pl.BlockSpec(w1.shape, lambda b, j: (0, 0)),
                  pl.BlockSpec((1, n1, F1_COLS), lambda b, j: (b, 0, j))],
        out_specs=pl.BlockSpec((1, 2 * n1, F1_COLS), lambda b, j: (b, 0, j)),
        out_shape=jax.ShapeDtypeStruct((B, 2 * n1, cols), bf16),
        compiler_params=_cparams(("parallel", "parallel")),
        name="fnet_stage1",
    )(w1, f2d)


def _fnet2_kernel(m_ref, a_ref, cc_ref, sc_ref, wf_ref, o_ref):
    n2 = a_ref.shape[3]
    for kk in range(a_ref.shape[2]):
        ari = jnp.concatenate([a_ref[0, 0, kk], a_ref[0, 1, kk]], axis=0)
        bri = _dot(m_ref[kk], ari)
        br = bri[:n2].astype(bf16)
        bi = bri[n2:].astype(bf16)
        for g in range(FNET_GROUPS):
            sl = slice(g * FNET_CH, (g + 1) * FNET_CH)
            fr = _dot(br[:, sl], cc_ref[...]) + _dot(bi[:, sl], sc_ref[...])
            fo = _dot(fr.astype(bf16), wf_ref[g])
            o_ref[0, :, kk * F_WIDTH + g * FNET_CH:kk * F_WIDTH + (g + 1) * FNET_CH] = fo.astype(bf16)


def _fnet2(a5, mtab, cc, sc, wfn):
    B, _, n1, n2, c = a5.shape
    return pl.pallas_call(
        _fnet2_kernel,
        grid=(B, n1 // F2_K1),
        in_specs=[pl.BlockSpec((F2_K1, 2 * n2, 2 * n2), lambda b, j: (j, 0, 0)),
                  pl.BlockSpec((1, 2, F2_K1, n2, c), lambda b, j: (b, 0, j, 0, 0)),
                  pl.BlockSpec(cc.shape, lambda b, j: (0, 0)),
                  pl.BlockSpec(sc.shape, lambda b, j: (0, 0)),
                  pl.BlockSpec(wfn.shape, lambda b, j: (0, 0, 0))],
        out_specs=pl.BlockSpec((1, n2, F2_K1 * c), lambda b, j: (b, 0, j)),
        out_shape=jax.ShapeDtypeStruct((B, n2, n1 * c), bf16),
        compiler_params=_cparams(("parallel", "parallel")),
        name="fnet_stage2",
    )(mtab, a5, cc, sc, wfn)


def _dft_tables(S):
    n1, n2 = S // FN2, FN2
    i1 = jnp.arange(n1, dtype=jnp.int32)
    ang1 = (2.0 * math.pi / n1) * ((i1[:, None] * i1[None, :]) % n1).astype(f32)
    w1 = jnp.concatenate([jnp.cos(ang1), -jnp.sin(ang1)], axis=0) * (n1 ** -0.5)
    k = i1[:, None, None] + n1 * jnp.arange(n2, dtype=jnp.int32)[None, :, None]
    nn = jnp.arange(n2, dtype=jnp.int32)[None, None, :]
    ang2 = (2.0 * math.pi / S) * ((k * nn) % S).astype(f32)
    c2 = jnp.cos(ang2) * (n2 ** -0.5)
    s2 = jnp.sin(ang2) * (n2 ** -0.5)
    mtab = jnp.concatenate([jnp.concatenate([c2, s2], axis=2),
                            jnp.concatenate([-s2, c2], axis=2)], axis=1)
    ic = jnp.arange(FNET_CH, dtype=jnp.int32)
    angc = (2.0 * math.pi / FNET_CH) * ((ic[:, None] * ic[None, :]) % FNET_CH).astype(f32)
    cc = jnp.cos(angc) * (FNET_CH ** -0.5)
    sc = jnp.sin(angc) * (FNET_CH ** -0.5)
    return w1.astype(bf16), mtab.astype(bf16), cc.astype(bf16), sc.astype(bf16)


def _memkv_kernel(mem_ref, w_ref, k_ref, v_ref):
    kv = _dot(mem_ref[0].astype(bf16), w_ref[...])
    k_ref[0] = kv[:, :D_MODEL].astype(bf16)
    v_ref[0] = kv[:, D_MODEL:].astype(bf16)


def _memkv(mem, w_mkv):
    B, M, D = mem.shape
    return pl.pallas_call(
        _memkv_kernel,
        grid=(B,),
        in_specs=[pl.BlockSpec((1, M, D), lambda b: (b, 0, 0)),
                  pl.BlockSpec(w_mkv.shape, lambda b: (0, 0))],
        out_specs=[pl.BlockSpec((1, M, D), lambda b: (b, 0, 0))] * 2,
        out_shape=[jax.ShapeDtypeStruct((B, M, D), bf16)] * 2,
        compiler_params=_cparams(("parallel",)),
        name="memkv",
    )(mem, w_mkv)


def _post_kernel(x_ref, o_ref, fo_ref, wo_ref, g1_ref, b1_ref, wmq_ref, km_ref, vm_ref, wmo_ref,
                 g2_ref, b2_ref, wrh_ref, wrl_ref, br_ref,
                 x2_ref, x2t_ref, idx_ref, gate_ref, rank_ref, cnt_ref, carry_ref):
    first = jnp.logical_and(pl.program_id(0) == 0, pl.program_id(1) == 0)

    @pl.when(first)
    def _():
        carry_ref[...] = jnp.zeros_like(carry_ref)

    n_sub = x_ref.shape[1] // POST_SUB
    gens = [_post_rows(slice(s * POST_SUB, (s + 1) * POST_SUB), x_ref, o_ref, fo_ref, wo_ref, g1_ref, b1_ref,
                       wmq_ref, km_ref, vm_ref, wmo_ref, g2_ref, b2_ref, wrh_ref, wrl_ref, br_ref, x2_ref, x2t_ref)
            for s in range(n_sub)]
    routed = [None] * n_sub
    active = list(range(n_sub))
    while active:
        for s in list(active):
            try:
                next(gens[s])
            except StopIteration as done:
                routed[s] = done.value
                active.remove(s)

    carry = carry_ref[...]
    for s in range(n_sub):
        rows = slice(s * POST_SUB, (s + 1) * POST_SUB)
        lane, idxs, gates, before, sel_sum = routed[s]
        idx_out = jnp.zeros((POST_SUB, LANES), f32)
        gate_out = jnp.zeros((POST_SUB, LANES), f32)
        rank_out = jnp.zeros((POST_SUB, LANES), f32)
        for k in range(TOP_K):
            rk = jnp.sum(jnp.where(lane == idxs[k], before + carry, 0.0), axis=-1, keepdims=True)
            idx_out = jnp.where(lane == float(k), idxs[k], idx_out)
            gate_out = jnp.where(lane == float(k), gates[k], gate_out)
            rank_out = jnp.where(lane == float(k), rk, rank_out)
        idx_ref[0, rows] = idx_out.astype(jnp.int32)
        gate_ref[0, rows] = gate_out
        rank_ref[0, rows] = rank_out.astype(jnp.int32)
        carry = carry + sel_sum
    carry_ref[...] = carry
    cnt_ref[...] = carry.astype(jnp.int32)


def _post_rows(rows, x_ref, o_ref, fo_ref, wo_ref, g1_ref, b1_ref, wmq_ref, km_ref, vm_ref, wmo_ref,
               g2_ref, b2_ref, wrh_ref, wrl_ref, br_ref, x2_ref, x2t_ref):
    x = x_ref[0, rows]
    tt = x.shape[0]
    h = _dot(o_ref[0, rows], wo_ref[:V_WIDTH, :]) + _dot(fo_ref[0, rows], wo_ref[V_WIDTH:, :])
    yield
    x1 = _layer_norm(DEEPNORM_ALPHA * x + h, g1_ref[...], b1_ref[...])
    qm = _dot(x1.astype(bf16), wmq_ref[...]).astype(bf16)
    yield
    heads = []
    for hh in range(MEM_HEADS):
        sl = slice(hh * MEM_DH, (hh + 1) * MEM_DH)
        s = _dot_nt(qm[:, sl], km_ref[0, :, sl]) * (MEM_DH ** -0.5)
        s = s - jnp.max(s, axis=-1, keepdims=True)
        e = jnp.exp(s)
        p = e / jnp.sum(e, axis=-1, keepdims=True)
        heads.append(_dot(p.astype(bf16), vm_ref[0, :, sl]))
        yield
    om = jnp.concatenate(heads, axis=-1).astype(bf16)
    h2 = _dot(om, wmo_ref[...])
    yield
    x2 = _layer_norm(DEEPNORM_ALPHA * x1 + h2, g2_ref[...], b2_ref[...])
    x2_ref[0, rows] = x2
    _store_token_tiles(x2t_ref, (0,), rows.start, x2)

    xh = x2.astype(bf16)
    xl = (x2 - xh.astype(f32)).astype(bf16)
    logits = (_dot(xh, wrh_ref[...]) + _dot(xl, wrh_ref[...]) + _dot(xh, wrl_ref[...])) + br_ref[...]
    yield

    lane = lax.broadcasted_iota(jnp.int32, (tt, LANES), 1).astype(f32)
    work = logits
    vals, idxs = [], []
    for _ in range(TOP_K):
        mx = jnp.max(work, axis=-1, keepdims=True)
        ix = jnp.min(jnp.where(work == mx, lane, float(LANES)), axis=-1, keepdims=True)
        vals.append(mx)
        idxs.append(ix)
        work = jnp.where(lane == ix, -jnp.inf, work)
    es = [jnp.exp(v - vals[0]) for v in vals]
    den = es[0] + es[1] + es[2] + es[3]
    gates = [e / den for e in es]
    yield

    sel = jnp.zeros((tt, LANES), f32)
    for ix in idxs:
        sel = sel + jnp.where(lane == ix, 1.0, 0.0)
    row = lax.broadcasted_iota(jnp.int32, (tt, tt), 0)
    col = lax.broadcasted_iota(jnp.int32, (tt, tt), 1)
    ltri = jnp.where(col < row, 1.0, 0.0).astype(bf16)
    before = _dot(ltri, sel.astype(bf16))
    return lane, idxs, gates, before, jnp.sum(sel, axis=0, keepdims=True)


def _post(x, o, fo, wo, g1, b1, wmq, km, vm, wmo, g2, b2, wrh, wrl, br):
    B, S, D = x.shape
    tok = lambda w: pl.BlockSpec((1, POST_TT, w), lambda b, i: (b, i, 0))
    const = lambda a: pl.BlockSpec(a.shape, lambda b, i: (0,) * a.ndim)
    mem = lambda a: pl.BlockSpec((1,) + a.shape[1:], lambda b, i: (b, 0, 0))
    return pl.pallas_call(
        _post_kernel,
        grid=(B, S // POST_TT),
        in_specs=[tok(D), tok(V_WIDTH), tok(F_WIDTH), const(wo), const(g1), const(b1), const(wmq),
                  mem(km), mem(vm), const(wmo), const(g2), const(b2), const(wrh), const(wrl), const(br)],
        out_specs=[tok(D), pl.BlockSpec((1, POST_TT * SUBLANES, LANES), lambda b, i: (b, i, 0)),
                   tok(LANES), tok(LANES), tok(LANES),
                   pl.BlockSpec((1, LANES), lambda b, i: (0, 0))],
        out_shape=[jax.ShapeDtypeStruct((B, S, D), f32),
                   jax.ShapeDtypeStruct((B, S * SUBLANES, LANES), f32),
                   jax.ShapeDtypeStruct((B, S, LANES), jnp.int32),
                   jax.ShapeDtypeStruct((B, S, LANES), f32),
                   jax.ShapeDtypeStruct((B, S, LANES), jnp.int32),
                   jax.ShapeDtypeStruct((1, LANES), jnp.int32)],
        scratch_shapes=[pltpu.VMEM((1, LANES), f32)],
        compiler_params=_cparams(("arbitrary", "arbitrary")),
        name="post_mixer_router",
    )(x, o, fo, wo, g1, b1, wmq, km, vm, wmo, g2, b2, wrh, wrl, br)


def _tile_rows(t):
    return pl.ds(pl.multiple_of(t * SUBLANES, SUBLANES), SUBLANES)


def _row_copy_out(x_ref, xs_ref, sem, t, d):
    return pltpu.make_async_copy(x_ref.at[_tile_rows(t)], xs_ref.at[_tile_rows(d)], sem)


def _zero_copy(zero_ref, xs_ref, sem, row):
    return pltpu.make_async_copy(
        zero_ref, xs_ref.at[pl.ds(pl.multiple_of(row * SUBLANES, SUBLANES), zero_ref.shape[0])], sem)


def _dispatch_kernel(zrow_ref, dest_ref, x_ref, xs_ref, zero_ref, zsem, sem):
    i = pl.program_id(0)
    rows_per_tile = TT * TOP_K

    @pl.when(i == 0)
    def _():
        zero_ref[...] = jnp.zeros_like(zero_ref)
        for j in range(zrow_ref.shape[0]):
            _zero_copy(zero_ref, xs_ref, zsem, zrow_ref[j]).start()
        for j in range(zrow_ref.shape[0]):
            _zero_copy(zero_ref, xs_ref, zsem, 0).wait()

    def issue(g, c):
        r0 = g * DMA_UNROLL
        dests = [dest_ref[0, 0, r0 * TOP_K + j] for j in range(DMA_UNROLL * TOP_K)]
        for u in range(DMA_UNROLL):
            for k in range(TOP_K):
                _row_copy_out(x_ref, xs_ref, sem, i * TT + r0 + u, dests[u * TOP_K + k]).start(priority=k % 2)
        return c

    lax.fori_loop(0, TT // DMA_UNROLL, issue, 0)

    def drain(g, c):
        for u in range(DMA_UNROLL * TOP_K):
            _row_copy_out(x_ref, xs_ref, sem, 0, 0).wait()
        return c

    @pl.when(i > 0)
    def _():
        lax.fori_loop(0, rows_per_tile // (DMA_UNROLL * TOP_K), drain, 0)

    @pl.when(i == pl.num_programs(0) - 1)
    def _():
        lax.fori_loop(0, rows_per_tile // (DMA_UNROLL * TOP_K), drain, 0)


def _dispatch(zrows, dest3, x2_tiles, R):
    T = x2_tiles.shape[0] // SUBLANES
    grid_spec = pltpu.PrefetchScalarGridSpec(
        num_scalar_prefetch=1,
        grid=(T // TT,),
        in_specs=[pl.BlockSpec((1, 1, TT * TOP_K), lambda i, z: (i, 0, 0), memory_space=pltpu.SMEM),
                  pl.BlockSpec(memory_space=pl.ANY)],
        out_specs=pl.BlockSpec(memory_space=pl.ANY),
        scratch_shapes=[pltpu.VMEM((EXPERT_BM * SUBLANES, LANES), f32),
                        pltpu.SemaphoreType.DMA, pltpu.SemaphoreType.DMA],
    )
    return pl.pallas_call(
        _dispatch_kernel,
        grid_spec=grid_spec,
        out_shape=jax.ShapeDtypeStruct((R * SUBLANES, LANES), f32),
        compiler_params=_cparams(("arbitrary",)),
        name="moe_dispatch",
    )(zrows, dest3, x2_tiles)


def _expert_kernel(be_ref, xs_ref, wgu_ref, bgu_ref, wd_ref, bd_ref, ys_ref):
    del be_ref
    bm = xs_ref.shape[0] // SUBLANES
    x = _load_token_tiles(xs_ref, (), bm)
    hb = _dot(x.astype(bf16), wgu_ref[0]) + bgu_ref[0]
    g = jnp.minimum(hb[:, :D_FF], SWIGLU_LIMIT)
    u = jnp.clip(hb[:, D_FF:], -SWIGLU_LIMIT, SWIGLU_LIMIT)
    a = (u + 1.0) * (g * jax.nn.sigmoid(g * SWIGLU_ALPHA))
    _store_token_tiles(ys_ref, (), 0, _dot(a.astype(bf16), wd_ref[0]) + bd_ref[0])


def _experts(block_e, xs, wgu, bgu, wd, bd):
    D = D_MODEL
    rows = EXPERT_BM * SUBLANES
    grid_spec = pltpu.PrefetchScalarGridSpec(
        num_scalar_prefetch=1,
        grid=(xs.shape[0] // rows,),
        in_specs=[pl.BlockSpec((rows, LANES), lambda i, be: (i, 0)),
                  pl.BlockSpec((1, D, 2 * D_FF), lambda i, be: (be[i], 0, 0)),
                  pl.BlockSpec((1, 1, 2 * D_FF), lambda i, be: (be[i], 0, 0)),
                  pl.BlockSpec((1, D_FF, D), lambda i, be: (be[i], 0, 0)),
                  pl.BlockSpec((1, 1, D), lambda i, be: (be[i], 0, 0))],
        out_specs=pl.BlockSpec((rows, LANES), lambda i, be: (i, 0)),
    )
    return pl.pallas_call(
        _expert_kernel,
        grid_spec=grid_spec,
        out_shape=jax.ShapeDtypeStruct(xs.shape, f32),
        compiler_params=_cparams(("arbitrary",)),
        name="moe_experts",
    )(block_e, xs, wgu, bgu, wd, bd)


def _row_copy_in(ys_ref, buf_ref, sems, slot, k, r, d):
    return pltpu.make_async_copy(ys_ref.at[_tile_rows(d)], buf_ref.at[slot, k, _tile_rows(r)], sems.at[slot])


def _combine_kernel(dest_ref, dest_next_ref, ys_ref, x2_ref, gate_ref, g3_ref, b3_ref, out_ref, buf_ref, sems):
    i = pl.program_id(0)
    slot = i % 2

    def start_tile(d_ref, s):
        def issue(g, c):
            r0 = g * DMA_UNROLL
            srcs = [d_ref[0, 0, r0 * TOP_K + j] for j in range(DMA_UNROLL * TOP_K)]
            for u in range(DMA_UNROLL):
                for k in range(TOP_K):
                    _row_copy_in(ys_ref, buf_ref, sems, s, k, r0 + u, srcs[u * TOP_K + k]).start(priority=k % 2)
            return c

        lax.fori_loop(0, TT // DMA_UNROLL, issue, 0)

    @pl.when(i == 0)
    def _():
        start_tile(dest_ref, 0)

    @pl.when(i + 1 < pl.num_programs(0))
    def _():
        start_tile(dest_next_ref, 1 - slot)

    def drain(g, c):
        for u in range(DMA_UNROLL * TOP_K):
            _row_copy_in(ys_ref, buf_ref, sems, slot, 0, 0, 0).wait()
        return c

    lax.fori_loop(0, TT // DMA_UNROLL, drain, 0)

    gates = gate_ref[...]
    y = gates[:, 0:1] * _load_token_tiles(buf_ref, (slot, 0), TT)
    for k in range(1, TOP_K):
        y = y + gates[:, k:k + 1] * _load_token_tiles(buf_ref, (slot, k), TT)
    out_ref[...] = _layer_norm(DEEPNORM_ALPHA * x2_ref[...] + y, g3_ref[...], b3_ref[...])


def _combine(dest3, ys, x2_flat, gates_flat, g3, b3):
    T, D = x2_flat.shape
    n_tiles = T // TT
    dest_spec = lambda off: pl.BlockSpec((1, 1, TT * TOP_K), lambda i: (jnp.minimum(i + off, n_tiles - 1), 0, 0),
                                         memory_space=pltpu.SMEM)
    return pl.pallas_call(
        _combine_kernel,
        grid=(n_tiles,),
        in_specs=[dest_spec(0), dest_spec(1),
                  pl.BlockSpec(memory_space=pl.ANY),
                  pl.BlockSpec((TT, D), lambda i: (i, 0)),
                  pl.BlockSpec((TT, LANES), lambda i: (i, 0)),
                  pl.BlockSpec(g3.shape, lambda i: (0, 0)),
                  pl.BlockSpec(b3.shape, lambda i: (0, 0))],
        out_specs=pl.BlockSpec((TT, D), lambda i: (i, 0)),
        out_shape=jax.ShapeDtypeStruct((T, D), f32),
        scratch_shapes=[pltpu.VMEM((2, TOP_K, TT * SUBLANES, LANES), f32), pltpu.SemaphoreType.DMA((2,))],
        compiler_params=_cparams(("arbitrary",)),
        name="moe_combine",
    )(dest3, dest3, ys, x2_flat, gates_flat, g3, b3)


def _rot_cols(w):
    d, n = w.shape
    w4 = w.reshape(d, n // DIFF_DH, 2, DIFF_DH // 2)
    return jnp.concatenate([-w4[:, :, 1], w4[:, :, 0]], axis=-1).reshape(d, n)


def _rope_tables(S):
    half = DIFF_DH // 2
    inv = 1.0 / (ROPE_THETA ** (jnp.arange(half, dtype=f32) / half))
    ang = jnp.arange(S, dtype=f32)[:, None] * inv[None, :]
    cos = jnp.concatenate([jnp.cos(ang), jnp.cos(ang)], -1)
    sin = jnp.concatenate([jnp.sin(ang), jnp.sin(ang)], -1)
    return (jnp.concatenate([cos, cos], -1), jnp.concatenate([sin, sin], -1), cos.T, sin.T)


def _prep_weights(w_in, w_fnet, w_o, w_mq, w_mkv, w_mo, w_router, b_router, w_gu, w_down):
    wq = w_in[:, :QK_WIDTH]
    wk = w_in[:, QK_WIDTH:2 * QK_WIDTH]
    wv = w_in[:, 2 * QK_WIDTH:2 * QK_WIDTH + V_WIDTH]
    wf = w_in[:, 2 * QK_WIDTH + V_WIDTH:]
    wqt = jnp.concatenate([wq, _rot_cols(wq)], axis=1).T.astype(bf16)
    wkk = jnp.concatenate([wk, _rot_cols(wk)], axis=1).astype(bf16)
    wvt = wv.T.astype(bf16)
    pad = LANES - N_EXPERTS
    wr = jnp.pad(w_router, ((0, 0), (0, pad)))
    wrh = wr.astype(bf16)
    wrl = (wr - wrh.astype(f32)).astype(bf16)
    br = jnp.pad(b_router, (0, pad), constant_values=NEG_BIG).reshape(1, LANES)
    return dict(wqt=wqt, wk=wkk, wvt=wvt, wf=wf.astype(bf16), wfn=w_fnet.astype(bf16),
                wo=w_o.astype(bf16), wmq=w_mq.astype(bf16), wmkv=w_mkv.astype(bf16),
                wmo=w_mo.astype(bf16), wrh=wrh, wrl=wrl, br=br,
                wgu=w_gu.astype(bf16), wd=w_down.astype(bf16))


def _trunk(x, mem, W, tables, lam_params, g_col, ln, b_gu, b_down):
    B, S, D = x.shape
    cosk, sink, cost, sint = tables["rope"]
    qz, kk, vt5, f = _inproj(x, W["wqt"], W["wk"], W["wvt"], W["wf"], cosk, sink, cost, sint)
    o = _attention(qz, kk, vt5, lam_params, g_col)

    n1 = S // FN2
    w1, mtab, cc, sc = tables["dft"]
    a = _fnet1(f.reshape(B, n1, FN2 * F_WIDTH), w1)
    fo = _fnet2(a.reshape(B, 2, n1, FN2, F_WIDTH), mtab, cc, sc, W["wfn"]).reshape(B, S, F_WIDTH)

    km, vm = _memkv(mem, W["wmkv"])
    x2, x2t, idx, gates, rank, counts = _post(x, o, fo, W["wo"], ln["g1"], ln["b1"], W["wmq"], km, vm,
                                         W["wmo"], ln["g2"], ln["b2"], W["wrh"], W["wrl"], W["br"])

    T = B * S
    N = T * TOP_K
    bm = EXPERT_BM
    counts = counts[0, :N_EXPERTS]
    padded = (counts + bm - 1) // bm * bm
    pstarts = jnp.cumsum(padded) - padded
    R = N + N_EXPERTS * bm
    nb = R // bm
    ends = jnp.cumsum(padded)
    block_e = jnp.minimum(
        jnp.sum((ends[None, :] <= (jnp.arange(nb, dtype=jnp.int32) * bm)[:, None]).astype(jnp.int32), axis=1),
        N_EXPERTS - 1).astype(jnp.int32)
    top_i = idx.reshape(T, LANES)[:, :TOP_K]
    onehot = top_i[:, :, None] == jnp.arange(N_EXPERTS, dtype=jnp.int32)[None, None, :]
    dest = jnp.sum(jnp.where(onehot, pstarts[None, None, :], 0), axis=-1) + rank.reshape(T, LANES)[:, :TOP_K]
    dest3 = dest.astype(jnp.int32).reshape(T // TT, 1, TT * TOP_K)

    last_blk = (nb - 1) * bm
    seg_last = jnp.where(padded > 0, pstarts + padded - bm, last_blk)
    tail = jnp.minimum(ends[-1] + jnp.arange(N_EXPERTS, dtype=jnp.int32) * bm, last_blk)
    zrows = jnp.concatenate([seg_last, tail]).astype(jnp.int32)

    x2f = x2.reshape(T, D)
    xs = _dispatch(zrows, dest3, x2t.reshape(T * SUBLANES, LANES), R)
    ys = _experts(block_e, xs, W["wgu"], b_gu, W["wd"], b_down)
    out = _combine(dest3, ys, x2f, gates.reshape(T, LANES), ln["g3"], ln["b3"])
    return out.reshape(B, S, D)


def kernel(x_prompt, x_sample, mem_prompt, mem_sample, w_in, lambda_q1, lambda_k1, lambda_q2, lambda_k2,
           subln_g, w_fnet, w_o, ln1_g, ln1_b, w_mq, w_mkv, w_mo, ln2_g, ln2_b, w_router, b_router,
           w_gu, b_gu, w_down, b_down, ln3_g, ln3_b):
    l = 0
    W = _prep_weights(w_in[l], w_fnet[l], w_o[l], w_mq[l], w_mkv[l], w_mo[l], w_router[l], b_router[l],
                      w_gu[l], w_down[l])
    lam_params = jnp.stack([lambda_q1[l], lambda_k1[l], lambda_q2[l], lambda_k2[l]], axis=0)
    g_col = subln_g[l].reshape(DIFF_VDIM, 1)
    row = lambda v: v[l].reshape(1, -1)
    ln = dict(g1=row(ln1_g), b1=row(ln1_b), g2=row(ln2_g), b2=row(ln2_b), g3=row(ln3_g), b3=row(ln3_b))
    bgu = b_gu[l].reshape(N_EXPERTS, 1, 2 * D_FF)
    bdn = b_down[l].reshape(N_EXPERTS, 1, D_MODEL)
    tables = {}
    outs = []
    for x, mem in ((x_prompt, mem_prompt), (x_sample, mem_sample)):
        S = x.shape[1]
        if S not in tables:
            tables[S] = dict(rope=_rope_tables(S), dft=_dft_tables(S))
        outs.append(_trunk(x, mem, W, tables[S], lam_params, g_col, ln, bgu, bdn))
    return tuple(outs)
```

```python
import functools
import math

import jax
import jax.numpy as jnp
import numpy as np
from jax import lax
from jax.experimental import pallas as pl
from jax.experimental.pallas import tpu as pltpu

D_MODEL = 1024
N_MEM = 256
MEM_HEADS = 4
MEM_DH = D_MODEL // MEM_HEADS
DIFF_HEADS = 4
DIFF_DH = 64
DIFF_VDIM = 2 * DIFF_DH
QK_WIDTH = DIFF_HEADS * 2 * DIFF_DH
V_WIDTH = DIFF_HEADS * DIFF_VDIM
FNET_GROUPS = 4
FNET_CH = 128
F_WIDTH = FNET_GROUPS * FNET_CH
ROPE_THETA = 10000.0
N_EXPERTS = 32
TOP_K = 4
D_FF = D_MODEL
SWIGLU_LIMIT = 7.0
SWIGLU_ALPHA = 1.702
DEPTH = 1
DEEPNORM_ALPHA = (2.0 * DEPTH) ** 0.25
LN_EPS = 1e-5
SUBLN_EPS = 1e-5
LAMBDA_INIT = 0.8 - 0.6 * math.exp(-0.3 * 0)
LOG2_E = math.log2(math.e)
BF16_SUBLANES = 16
V_ROWS = DIFF_VDIM + BF16_SUBLANES

LANES = 128
SUBLANES = 8
VMEM_LIMIT = 56 * 1024 * 1024

TS_IN = 512
TQ = 256
ATTN_PV_LAG = 3
ATTN_UNROLL = 12
FN2 = 128
F1_COLS = 4096
F2_K1 = 8
POST_TT = 1024
POST_SUB = 256
TT = 256
EXPERT_BM = 256
DMA_UNROLL = 4
NEG_BIG = -1e30

bf16 = jnp.bfloat16
f32 = jnp.float32


def _cparams(sem):
    return pltpu.CompilerParams(dimension_semantics=sem, vmem_limit_bytes=VMEM_LIMIT)


def _dot(a, b):
    return jnp.dot(a, b, preferred_element_type=f32)


def _dot_nt(a, b):
    return lax.dot_general(a, b, (((1,), (1,)), ((), ())), preferred_element_type=f32)


def _load_token_tiles(ref, lead, n):
    return jnp.concatenate([ref[lead + (pl.ds(j, n, stride=SUBLANES),)] for j in range(D_MODEL // LANES)], axis=-1)


def _store_token_tiles(ref, lead, row0, v):
    for j in range(D_MODEL // LANES):
        ref[lead + (pl.ds(row0 * SUBLANES + j, v.shape[0], stride=SUBLANES),)] = v[:, j * LANES:(j + 1) * LANES]


def _layer_norm(v, g, b):
    mu = jnp.mean(v, axis=-1, keepdims=True)
    var = jnp.mean(jnp.square(v - mu), axis=-1, keepdims=True)
    return (v - mu) * lax.rsqrt(var + LN_EPS) * g + b


def _inproj_kernel(x_ref, wqt_ref, wk_ref, wvt_ref, wf_ref, cosk_ref, sink_ref, cost_ref, sint_ref,
                   qz_ref, kk_ref, vt_ref, f_ref):
    xb = x_ref[0].astype(bf16)
    hk = _dot(xb, wk_ref[...])
    ck = cosk_ref[...]
    sk = sink_ref[...]
    for h in range(DIFF_HEADS):
        a = hk[:, h * LANES:(h + 1) * LANES]
        r = hk[:, QK_WIDTH + h * LANES:QK_WIDTH + (h + 1) * LANES]
        kk_ref[0, h] = (a * ck + r * sk).astype(bf16)
    hq = _dot_nt(wqt_ref[...], xb)
    ct = cost_ref[...]
    st = sint_ref[...]
    scale = DIFF_DH ** -0.5 * LOG2_E
    zeros = jnp.zeros((DIFF_DH, xb.shape[0]), bf16)
    for h in range(DIFF_HEADS):
        for m in range(2):
            c = h * 2 + m
            a = hq[c * DIFF_DH:(c + 1) * DIFF_DH]
            r = hq[QK_WIDTH + c * DIFF_DH:QK_WIDTH + (c + 1) * DIFF_DH]
            q = ((a * ct + r * st) * scale).astype(bf16)
            if m == 0:
                qz_ref[0, h, 0, :DIFF_DH, :] = q
                qz_ref[0, h, 0, DIFF_DH:, :] = zeros
            else:
                qz_ref[0, h, 1, :DIFF_DH, :] = zeros
                qz_ref[0, h, 1, DIFF_DH:, :] = q
    hv = _dot_nt(wvt_ref[...], xb)
    sub = lax.broadcasted_iota(jnp.int32, (V_ROWS - DIFF_VDIM, xb.shape[0]), 0)
    ones_rows = jnp.where(sub == 0, 1.0, 0.0).astype(bf16)
    for h in range(DIFF_HEADS):
        vt_ref[0, h, 0, :DIFF_VDIM, :] = hv[h * DIFF_VDIM:(h + 1) * DIFF_VDIM].astype(bf16)
        vt_ref[0, h, 0, DIFF_VDIM:, :] = ones_rows
    f_ref[0] = _dot(xb, wf_ref[...]).astype(bf16)


def _inproj(x, wqt, wk, wvt, wf, cosk, sink, cost, sint):
    B, S, D = x.shape
    ts = TS_IN
    nchunk = S // ts
    const = lambda shape: pl.BlockSpec(shape, lambda b, i: (0,) * len(shape))
    return pl.pallas_call(
        _inproj_kernel,
        grid=(B, nchunk),
        in_specs=[
            pl.BlockSpec((1, ts, D), lambda b, i: (b, i, 0)),
            const(wqt.shape), const(wk.shape), const(wvt.shape), const(wf.shape),
            pl.BlockSpec((ts, LANES), lambda b, i: (i, 0)),
            pl.BlockSpec((ts, LANES), lambda b, i: (i, 0)),
            pl.BlockSpec((DIFF_DH, ts), lambda b, i: (0, i)),
            pl.BlockSpec((DIFF_DH, ts), lambda b, i: (0, i)),
        ],
        out_specs=[
            pl.BlockSpec((1, DIFF_HEADS, 2, LANES, ts), lambda b, i: (b, 0, 0, 0, i)),
            pl.BlockSpec((1, DIFF_HEADS, ts, LANES), lambda b, i: (b, 0, i, 0)),
            pl.BlockSpec((1, DIFF_HEADS, 1, V_ROWS, ts), lambda b, i: (b, 0, i, 0, 0)),
            pl.BlockSpec((1, ts, F_WIDTH), lambda b, i: (b, i, 0)),
        ],
        out_shape=[
            jax.ShapeDtypeStruct((B, DIFF_HEADS, 2, LANES, S), bf16),
            jax.ShapeDtypeStruct((B, DIFF_HEADS, S, LANES), bf16),
            jax.ShapeDtypeStruct((B, DIFF_HEADS, nchunk, V_ROWS, ts), bf16),
            jax.ShapeDtypeStruct((B, S, F_WIDTH), bf16),
        ],
        compiler_params=_cparams(("parallel", "parallel")),
        name="inproj",
    )(x, wqt, wk, wvt, wf, cosk, sink, cost, sint)


def _attn_kernel(qz_ref, kk_ref, vt_ref, lam_ref, g_ref, o_ref, s_scr, p_scr, acc_ref):
    nchunk = vt_ref.shape[2]
    tk = vt_ref.shape[4]
    tq = qz_ref.shape[4]
    acc_ref[...] = jnp.zeros_like(acc_ref)

    def scores(j, slot):
        kt = kk_ref[0, 0, pl.ds(pl.multiple_of(j * tk, tk), tk), :]
        cms = []
        for mp in range(2):
            s = _dot(kt, qz_ref[0, 0, mp])
            s_scr[slot, mp] = s
            cms.append(jnp.max(s, axis=0, keepdims=True))
        return tuple(cms)

    def probs(s_slot, p_slot, cms, ms):
        m_new, alphas = [], []
        for mp in range(2):
            m = jnp.maximum(ms[mp], cms[mp])
            alphas.append(jnp.exp2(ms[mp] - m))
            p_scr[p_slot, mp] = jnp.exp2(s_scr[s_slot, mp] - m).astype(bf16)
            m_new.append(m)
        return tuple(m_new), tuple(alphas)

    def accumulate(j, slot, alphas):
        vt = vt_ref[0, 0, j]
        for mp in range(2):
            acc_ref[mp] = acc_ref[mp] * alphas[mp] + _dot(vt, p_scr[slot, mp])

    lag = p_scr.shape[0]
    unroll = ATTN_UNROLL
    assert lag >= 2 and unroll % 2 == 0 and unroll % lag == 0 and nchunk >= lag

    def tick(t, r, ms, cm_prev, als):
        peeled = isinstance(t, int)
        cm = cm_prev
        if not peeled or t < nchunk:
            cm = scores(t, r % 2)
        if not peeled or 1 <= t <= nchunk:
            ms, al_new = probs((r - 1) % 2, (r - 1) % lag, cm_prev, ms)
            als = als + (al_new,)
        if not peeled or t >= lag:
            accumulate(t - lag, (r - lag) % lag, als[0])
            als = als[1:]
        return ms, cm, als

    first = lag + (nchunk - lag) % unroll
    ms = (jnp.full((1, tq), NEG_BIG, f32),) * 2
    cm, als = None, ()
    for t in range(first):
        ms, cm, als = tick(t, t % unroll, ms, cm, als)

    def ticks(it, carry):
        ms, cm, als = carry
        t0 = first + it * unroll
        for u in range(unroll):
            ms, cm, als = tick(t0 + u, (first + u) % unroll, ms, cm, als)
        return ms, cm, als

    ms, cm, als = lax.fori_loop(0, (nchunk - first) // unroll, ticks, (ms, cm, als))
    for t in range(nchunk, nchunk + lag):
        ms, cm, als = tick(t, t % unroll, ms, cm, als)

    lp = lam_ref[...]
    lam = (jnp.exp(jnp.sum(lp[0:1] * lp[1:2], axis=1, keepdims=True))
           - jnp.exp(jnp.sum(lp[2:3] * lp[3:4], axis=1, keepdims=True)) + LAMBDA_INIT)
    a0 = acc_ref[0]
    a1 = acc_ref[1]
    o = (a0[:DIFF_VDIM] / a0[DIFF_VDIM:DIFF_VDIM + 1]
         - lam * (a1[:DIFF_VDIM] / a1[DIFF_VDIM:DIFF_VDIM + 1]))
    ms2 = jnp.mean(jnp.square(o), axis=0, keepdims=True)
    y = o * lax.rsqrt(ms2 + SUBLN_EPS) * g_ref[...] * (1.0 - LAMBDA_INIT)
    o_ref[0] = y.T.astype(bf16)


def _attention(qz, kk, vt5, lam_params, g_col):
    B, H, _, _, S = qz.shape
    nchunk, tk = vt5.shape[2], vt5.shape[4]
    return pl.pallas_call(
        _attn_kernel,
        grid=(B, H, S // TQ),
        in_specs=[
            pl.BlockSpec((1, 1, 2, LANES, TQ), lambda b, h, i: (b, h, 0, 0, i)),
            pl.BlockSpec((1, 1, S, LANES), lambda b, h, i: (b, h, 0, 0)),
            pl.BlockSpec((1, 1, nchunk, V_ROWS, tk), lambda b, h, i: (b, h, 0, 0, 0)),
            pl.BlockSpec(lam_params.shape, lambda b, h, i: (0, 0)),
            pl.BlockSpec(g_col.shape, lambda b, h, i: (0, 0)),
        ],
        out_specs=pl.BlockSpec((1, TQ, DIFF_VDIM), lambda b, h, i: (b, i, h)),
        out_shape=jax.ShapeDtypeStruct((B, S, V_WIDTH), bf16),
        scratch_shapes=[pltpu.VMEM((2, 2, tk, TQ), f32),
                        pltpu.VMEM((ATTN_PV_LAG, 2, tk, TQ), bf16),
                        pltpu.VMEM((2, V_ROWS, TQ), f32)],
        compiler_params=_cparams(("parallel", "parallel", "parallel")),
        name="diffattn",
    )(qz, kk, vt5, lam_params, g_col)


def _fnet1_kernel(w1_ref, f_ref, a_ref):
    a_ref[0] = _dot(w1_ref[...], f_ref[0]).astype(bf16)


def _fnet1(f2d, w1):
    B, n1, cols = f2d.shape
    return pl.pallas_call(
        _fnet1_kernel,
        grid=(B, cols // F1_COLS),
        in_specs=[pl.BlockSpec(w1.shape, lambda b, j: (0, 0)),
                  pl.BlockSpec((1, n1, F1_COLS), lambda b, j: (b, 0, j))],
        out_specs=pl.BlockSpec((1, 2 * n1, F1_COLS), lambda b, j: (b, 0, j)),
        out_shape=jax.ShapeDtypeStruct((B, 2 * n1, cols), bf16),
        compiler_params=_cparams(("parallel", "parallel")),
        name="fnet_stage1",
    )(w1, f2d)


def _fnet2_kernel(m_ref, a_ref, cc_ref, sc_ref, wf_ref, o_ref):
    n2 = a_ref.shape[3]
    nk = a_ref.shape[2]
    groups = [slice(g * FNET_CH, (g + 1) * FNET_CH) for g in range(FNET_GROUPS)]
    bris = [_dot(m_ref[kk], jnp.concatenate([a_ref[0, 0, kk], a_ref[0, 1, kk]], axis=0))
            for kk in range(nk)]
    brs = [b[:n2].astype(bf16) for b in bris]
    bis = [b[n2:].astype(bf16) for b in bris]
    frs = [[(_dot(brs[kk][:, sl], cc_ref[...]) + _dot(bis[kk][:, sl], sc_ref[...])).astype(bf16) for sl in groups]
           for kk in range(nk)]
    for kk in range(nk):
        for g, sl in enumerate(groups):
            fo = _dot(frs[kk][g], wf_ref[g])
            o_ref[0, :, kk * F_WIDTH + g * FNET_CH:kk * F_WIDTH + (g + 1) * FNET_CH] = fo.astype(bf16)


def _fnet2(a5, mtab, cc, sc, wfn):
    B, _, n1, n2, c = a5.shape
    return pl.pallas_call(
        _fnet2_kernel,
        grid=(B, n1 // F2_K1),
        in_specs=[pl.BlockSpec((F2_K1, 2 * n2, 2 * n2), lambda b, j: (j, 0, 0)),
                  pl.BlockSpec((1, 2, F2_K1, n2, c), lambda b, j: (b, 0, j, 0, 0)),
                  pl.BlockSpec(cc.shape, lambda b, j: (0, 0)),
                  pl.BlockSpec(sc.shape, lambda b, j: (0, 0)),
                  pl.BlockSpec(wfn.shape, lambda b, j: (0, 0, 0))],
        out_specs=pl.BlockSpec((1, n2, F2_K1 * c), lambda b, j: (b, 0, j)),
        out_shape=jax.ShapeDtypeStruct((B, n2, n1 * c), bf16),
        compiler_params=_cparams(("parallel", "parallel")),
        name="fnet_stage2",
    )(mtab, a5, cc, sc, wfn)


def _dft_tables(S):
    n1, n2 = S // FN2, FN2
    i1 = jnp.arange(n1, dtype=jnp.int32)
    ang1 = (2.0 * math.pi / n1) * ((i1[:, None] * i1[None, :]) % n1).astype(f32)
    w1 = jnp.concatenate([jnp.cos(ang1), -jnp.sin(ang1)], axis=0) * (n1 ** -0.5)
    k = i1[:, None, None] + n1 * jnp.arange(n2, dtype=jnp.int32)[None, :, None]
    nn = jnp.arange(n2, dtype=jnp.int32)[None, None, :]
    ang2 = (2.0 * math.pi / S) * ((k * nn) % S).astype(f32)
    c2 = jnp.cos(ang2) * (n2 ** -0.5)
    s2 = jnp.sin(ang2) * (n2 ** -0.5)
    mtab = jnp.concatenate([jnp.concatenate([c2, s2], axis=2),
                            jnp.concatenate([-s2, c2], axis=2)], axis=1)
    ic = jnp.arange(FNET_CH, dtype=jnp.int32)
    angc = (2.0 * math.pi / FNET_CH) * ((ic[:, None] * ic[None, :]) % FNET_CH).astype(f32)
    cc = jnp.cos(angc) * (FNET_CH ** -0.5)
    sc = jnp.sin(angc) * (FNET_CH ** -0.5)
    return w1.astype(bf16), mtab.astype(bf16), cc.astype(bf16), sc.astype(bf16)


def _memkv_kernel(mem_ref, w_ref, k_ref, v_ref):
    kv = _dot(mem_ref[0].astype(bf16), w_ref[...])
    k_ref[0] = kv[:, :D_MODEL].astype(bf16)
    v_ref[0] = kv[:, D_MODEL:].astype(bf16)


def _memkv(mem, w_mkv):
    B, M, D = mem.shape
    return pl.pallas_call(
        _memkv_kernel,
        grid=(B,),
        in_specs=[pl.BlockSpec((1, M, D), lambda b: (b, 0, 0)),
                  pl.BlockSpec(w_mkv.shape, lambda b: (0, 0))],
        out_specs=[pl.BlockSpec((1, M, D), lambda b: (b, 0, 0))] * 2,
        out_shape=[jax.ShapeDtypeStruct((B, M, D), bf16)] * 2,
        compiler_params=_cparams(("parallel",)),
        name="memkv",
    )(mem, w_mkv)


def _post_kernel(x_ref, o_ref, fo_ref, wo_ref, g1_ref, b1_ref, wmq_ref, km_ref, vm_ref, wmo_ref,
                 g2_ref, b2_ref, wrh_ref, wrl_ref, br_ref,
                 x2_ref, x2t_ref, idx_ref, gate_ref, rank_ref, cnt_ref, carry_ref):
    first = jnp.logical_and(pl.program_id(0) == 0, pl.program_id(1) == 0)

    @pl.when(first)
    def _():
        carry_ref[...] = jnp.zeros_like(carry_ref)

    n_sub = x_ref.shape[1] // POST_SUB
    gens = [_post_rows(slice(s * POST_SUB, (s + 1) * POST_SUB), x_ref, o_ref, fo_ref, wo_ref, g1_ref, b1_ref,
                       wmq_ref, km_ref, vm_ref, wmo_ref, g2_ref, b2_ref, wrh_ref, wrl_ref, br_ref, x2_ref, x2t_ref)
            for s in range(n_sub)]
    routed = [None] * n_sub
    active = list(range(n_sub))
    while active:
        for s in list(active):
            try:
                next(gens[s])
            except StopIteration as done:
                routed[s] = done.value
                active.remove(s)

    carry = carry_ref[...]
    for s in range(n_sub):
        rows = slice(s * POST_SUB, (s + 1) * POST_SUB)
        lane, idxs, gates, before, sel_sum = routed[s]
        idx_out = jnp.zeros((POST_SUB, LANES), f32)
        gate_out = jnp.zeros((POST_SUB, LANES), f32)
        rank_out = jnp.zeros((POST_SUB, LANES), f32)
        for k in range(TOP_K):
            rk = jnp.sum(jnp.where(lane == idxs[k], before + carry, 0.0), axis=-1, keepdims=True)
            idx_out = jnp.where(lane == float(k), idxs[k], idx_out)
            gate_out = jnp.where(lane == float(k), gates[k], gate_out)
            rank_out = jnp.where(lane == float(k), rk, rank_out)
        idx_ref[0, rows] = idx_out.astype(jnp.int32)
        gate_ref[0, rows] = gate_out
        rank_ref[0, rows] = rank_out.astype(jnp.int32)
        carry = carry + sel_sum
    carry_ref[...] = carry
    cnt_ref[...] = carry.astype(jnp.int32)


def _post_rows(rows, x_ref, o_ref, fo_ref, wo_ref, g1_ref, b1_ref, wmq_ref, km_ref, vm_ref, wmo_ref,
               g2_ref, b2_ref, wrh_ref, wrl_ref, br_ref, x2_ref, x2t_ref):
    x = x_ref[0, rows]
    tt = x.shape[0]
    h = _dot(o_ref[0, rows], wo_ref[:V_WIDTH, :]) + _dot(fo_ref[0, rows], wo_ref[V_WIDTH:, :])
    yield
    x1 = _layer_norm(DEEPNORM_ALPHA * x + h, g1_ref[...], b1_ref[...])
    qm = _dot(x1.astype(bf16), wmq_ref[...]).astype(bf16)
    yield
    heads = []
    for hh in range(MEM_HEADS):
        sl = slice(hh * MEM_DH, (hh + 1) * MEM_DH)
        s = _dot_nt(qm[:, sl], km_ref[0, :, sl]) * (MEM_DH ** -0.5)
        s = s - jnp.max(s, axis=-1, keepdims=True)
        e = jnp.exp(s)
        p = e / jnp.sum(e, axis=-1, keepdims=True)
        heads.append(_dot(p.astype(bf16), vm_ref[0, :, sl]))
        yield
    om = jnp.concatenate(heads, axis=-1).astype(bf16)
    h2 = _dot(om, wmo_ref[...])
    yield
    x2 = _layer_norm(DEEPNORM_ALPHA * x1 + h2, g2_ref[...], b2_ref[...])
    x2_ref[0, rows] = x2
    _store_token_tiles(x2t_ref, (0,), rows.start, x2)

    xh = x2.astype(bf16)
    xl = (x2 - xh.astype(f32)).astype(bf16)
    logits = (_dot(xh, wrh_ref[...]) + _dot(xl, wrh_ref[...]) + _dot(xh, wrl_ref[...])) + br_ref[...]
    yield

    lane = lax.broadcasted_iota(jnp.int32, (tt, LANES), 1).astype(f32)
    work = logits
    vals, idxs = [], []
    for _ in range(TOP_K):
        mx = jnp.max(work, axis=-1, keepdims=True)
        ix = jnp.min(jnp.where(work == mx, lane, float(LANES)), axis=-1, keepdims=True)
        vals.append(mx)
        idxs.append(ix)
        work = jnp.where(lane == ix, -jnp.inf, work)
    es = [jnp.exp(v - vals[0]) for v in vals]
    den = es[0] + es[1] + es[2] + es[3]
    gates = [e / den for e in es]
    yield

    sel = jnp.zeros((tt, LANES), f32)
    for ix in idxs:
        sel = sel + jnp.where(lane == ix, 1.0, 0.0)
    row = lax.broadcasted_iota(jnp.int32, (tt, tt), 0)
    col = lax.broadcasted_iota(jnp.int32, (tt, tt), 1)
    ltri = jnp.where(col < row, 1.0, 0.0).astype(bf16)
    before = _dot(ltri, sel.astype(bf16))
    return lane, idxs, gates, before, jnp.sum(sel, axis=0, keepdims=True)


def _post(x, o, fo, wo, g1, b1, wmq, km, vm, wmo, g2, b2, wrh, wrl, br):
    B, S, D = x.shape
    tok = lambda w: pl.BlockSpec((1, POST_TT, w), lambda b, i: (b, i, 0))
    const = lambda a: pl.BlockSpec(a.shape, lambda b, i: (0,) * a.ndim)
    mem = lambda a: pl.BlockSpec((1,) + a.shape[1:], lambda b, i: (b, 0, 0))
    return pl.pallas_call(
        _post_kernel,
        grid=(B, S // POST_TT),
        in_specs=[tok(D), tok(V_WIDTH), tok(F_WIDTH), const(wo), const(g1), const(b1), const(wmq),
                  mem(km), mem(vm), const(wmo), const(g2), const(b2), const(wrh), const(wrl), const(br)],
        out_specs=[tok(D), pl.BlockSpec((1, POST_TT * SUBLANES, LANES), lambda b, i: (b, i, 0)),
                   tok(LANES), tok(LANES), tok(LANES),
                   pl.BlockSpec((1, LANES), lambda b, i: (0, 0))],
        out_shape=[jax.ShapeDtypeStruct((B, S, D), f32),
                   jax.ShapeDtypeStruct((B, S * SUBLANES, LANES), f32),
                   jax.ShapeDtypeStruct((B, S, LANES), jnp.int32),
                   jax.ShapeDtypeStruct((B, S, LANES), f32),
                   jax.ShapeDtypeStruct((B, S, LANES), jnp.int32),
                   jax.ShapeDtypeStruct((1, LANES), jnp.int32)],
        scratch_shapes=[pltpu.VMEM((1, LANES), f32)],
        compiler_params=_cparams(("arbitrary", "arbitrary")),
        name="post_mixer_router",
    )(x, o, fo, wo, g1, b1, wmq, km, vm, wmo, g2, b2, wrh, wrl, br)


def _tile_rows(t):
    return pl.ds(pl.multiple_of(t * SUBLANES, SUBLANES), SUBLANES)


def _row_copy_out(x_ref, xs_ref, sem, t, d):
    return pltpu.make_async_copy(x_ref.at[_tile_rows(t)], xs_ref.at[_tile_rows(d)], sem)


def _zero_copy(zero_ref, xs_ref, sem, row):
    return pltpu.make_async_copy(
        zero_ref, xs_ref.at[pl.ds(pl.multiple_of(row * SUBLANES, SUBLANES), zero_ref.shape[0])], sem)


def _dispatch_kernel(zrow_ref, dest_ref, x_ref, xs_ref, zero_ref, zsem, sem):
    i = pl.program_id(0)
    rows_per_tile = TT * TOP_K

    @pl.when(i == 0)
    def _():
        zero_ref[...] = jnp.zeros_like(zero_ref)
        for j in range(zrow_ref.shape[0]):
            _zero_copy(zero_ref, xs_ref, zsem, zrow_ref[j]).start()
        for j in range(zrow_ref.shape[0]):
            _zero_copy(zero_ref, xs_ref, zsem, 0).wait()

    def issue(g, c):
        r0 = g * DMA_UNROLL
        dests = [dest_ref[0, 0, r0 * TOP_K + j] for j in range(DMA_UNROLL * TOP_K)]
        for u in range(DMA_UNROLL):
            for k in range(TOP_K):
                _row_copy_out(x_ref, xs_ref, sem, r0 + u, dests[u * TOP_K + k]).start(priority=k % 2)
        return c

    lax.fori_loop(0, TT // DMA_UNROLL, issue, 0)

    def drain(g, c):
        for u in range(DMA_UNROLL * TOP_K):
            _row_copy_out(x_ref, xs_ref, sem, 0, 0).wait()
        return c

    lax.fori_loop(0, rows_per_tile // (DMA_UNROLL * TOP_K), drain, 0)


def _dispatch(zrows, dest3, x2_tiles, R):
    T = x2_tiles.shape[0] // SUBLANES
    grid_spec = pltpu.PrefetchScalarGridSpec(
        num_scalar_prefetch=1,
        grid=(T // TT,),
        in_specs=[pl.BlockSpec((1, 1, TT * TOP_K), lambda i, z: (i, 0, 0), memory_space=pltpu.SMEM),
                  pl.BlockSpec((TT * SUBLANES, LANES), lambda i, z: (i, 0))],
        out_specs=pl.BlockSpec(memory_space=pl.ANY),
        scratch_shapes=[pltpu.VMEM((EXPERT_BM * SUBLANES, LANES), f32),
                        pltpu.SemaphoreType.DMA, pltpu.SemaphoreType.DMA],
    )
    return pl.pallas_call(
        _dispatch_kernel,
        grid_spec=grid_spec,
        out_shape=jax.ShapeDtypeStruct((R * SUBLANES, LANES), f32),
        compiler_params=_cparams(("arbitrary",)),
        name="moe_dispatch",
    )(zrows, dest3, x2_tiles)


def _expert_kernel(be_ref, xs_ref, wgu_ref, bgu_ref, wd_ref, bd_ref, ys_ref):
    del be_ref
    bm = xs_ref.shape[0] // SUBLANES
    x = _load_token_tiles(xs_ref, (), bm)
    hb = _dot(x.astype(bf16), wgu_ref[0]) + bgu_ref[0]
    g = jnp.minimum(hb[:, :D_FF], SWIGLU_LIMIT)
    u = jnp.clip(hb[:, D_FF:], -SWIGLU_LIMIT, SWIGLU_LIMIT)
    a = (u + 1.0) * (g * jax.nn.sigmoid(g * SWIGLU_ALPHA))
    _store_token_tiles(ys_ref, (), 0, _dot(a.astype(bf16), wd_ref[0]) + bd_ref[0])


def _experts(block_e, xs, wgu, bgu, wd, bd):
    D = D_MODEL
    rows = EXPERT_BM * SUBLANES
    grid_spec = pltpu.PrefetchScalarGridSpec(
        num_scalar_prefetch=1,
        grid=(xs.shape[0] // rows,),
        in_specs=[pl.BlockSpec((rows, LANES), lambda i, be: (i, 0)),
                  pl.BlockSpec((1, D, 2 * D_FF), lambda i, be: (be[i], 0, 0)),
                  pl.BlockSpec((1, 1, 2 * D_FF), lambda i, be: (be[i], 0, 0)),
                  pl.BlockSpec((1, D_FF, D), lambda i, be: (be[i], 0, 0)),
                  pl.BlockSpec((1, 1, D), lambda i, be: (be[i], 0, 0))],
        out_specs=pl.BlockSpec((rows, LANES), lambda i, be: (i, 0)),
    )
    return pl.pallas_call(
        _expert_kernel,
        grid_spec=grid_spec,
        out_shape=jax.ShapeDtypeStruct(xs.shape, f32),
        compiler_params=_cparams(("arbitrary",)),
        name="moe_experts",
    )(block_e, xs, wgu, bgu, wd, bd)


def _row_copy_in(ys_ref, buf_ref, sems, slot, k, r, d):
    return pltpu.make_async_copy(ys_ref.at[_tile_rows(d)], buf_ref.at[slot, k, _tile_rows(r)], sems.at[slot])


def _combine_kernel(dest_ref, dest_next_ref, ys_ref, x2_ref, gate_ref, g3_ref, b3_ref, out_ref, buf_ref, sems):
    i = pl.program_id(0)
    slot = i % 2

    def start_tile(d_ref, s):
        def issue(g, c):
            r0 = g * DMA_UNROLL
            srcs = [d_ref[0, 0, r0 * TOP_K + j] for j in range(DMA_UNROLL * TOP_K)]
            for u in range(DMA_UNROLL):
                for k in range(TOP_K):
                    _row_copy_in(ys_ref, buf_ref, sems, s, k, r0 + u, srcs[u * TOP_K + k]).start(priority=k % 2)
            return c

        lax.fori_loop(0, TT // DMA_UNROLL, issue, 0)

    @pl.when(i == 0)
    def _():
        start_tile(dest_ref, 0)

    @pl.when(i + 1 < pl.num_programs(0))
    def _():
        start_tile(dest_next_ref, 1 - slot)

    def drain(g, c):
        for u in range(DMA_UNROLL * TOP_K):
            _row_copy_in(ys_ref, buf_ref, sems, slot, 0, 0, 0).wait()
        return c

    lax.fori_loop(0, TT // DMA_UNROLL, drain, 0)

    gates = gate_ref[...]
    y = gates[:, 0:1] * _load_token_tiles(buf_ref, (slot, 0), TT)
    for k in range(1, TOP_K):
        y = y + gates[:, k:k + 1] * _load_token_tiles(buf_ref, (slot, k), TT)
    out_ref[...] = _layer_norm(DEEPNORM_ALPHA * x2_ref[...] + y, g3_ref[...], b3_ref[...])


def _combine(dest3, ys, x2_flat, gates_flat, g3, b3):
    T, D = x2_flat.shape
    n_tiles = T // TT
    dest_spec = lambda off: pl.BlockSpec((1, 1, TT * TOP_K), lambda i: (jnp.minimum(i + off, n_tiles - 1), 0, 0),
                                         memory_space=pltpu.SMEM)
    return pl.pallas_call(
        _combine_kernel,
        grid=(n_tiles,),
        in_specs=[dest_spec(0), dest_spec(1),
                  pl.BlockSpec(memory_space=pl.ANY),
                  pl.BlockSpec((TT, D), lambda i: (i, 0)),
                  pl.BlockSpec((TT, LANES), lambda i: (i, 0)),
                  pl.BlockSpec(g3.shape, lambda i: (0, 0)),
                  pl.BlockSpec(b3.shape, lambda i: (0, 0))],
        out_specs=pl.BlockSpec((TT, D), lambda i: (i, 0)),
        out_shape=jax.ShapeDtypeStruct((T, D), f32),
        scratch_shapes=[pltpu.VMEM((2, TOP_K, TT * SUBLANES, LANES), f32), pltpu.SemaphoreType.DMA((2,))],
        compiler_params=_cparams(("arbitrary",)),
        name="moe_combine",
    )(dest3, dest3, ys, x2_flat, gates_flat, g3, b3)


def _rot_cols(w):
    d, n = w.shape
    w4 = w.reshape(d, n // DIFF_DH, 2, DIFF_DH // 2)
    return jnp.concatenate([-w4[:, :, 1], w4[:, :, 0]], axis=-1).reshape(d, n)


def _rope_tables(S):
    half = DIFF_DH // 2
    inv = 1.0 / (ROPE_THETA ** (jnp.arange(half, dtype=f32) / half))
    ang = jnp.arange(S, dtype=f32)[:, None] * inv[None, :]
    cos = jnp.concatenate([jnp.cos(ang), jnp.cos(ang)], -1)
    sin = jnp.concatenate([jnp.sin(ang), jnp.sin(ang)], -1)
    return (jnp.concatenate([cos, cos], -1), jnp.concatenate([sin, sin], -1), cos.T, sin.T)


def _prep_weights(w_in, w_fnet, w_o, w_mq, w_mkv, w_mo, w_router, b_router, w_gu, w_down):
    wq = w_in[:, :QK_WIDTH]
    wk = w_in[:, QK_WIDTH:2 * QK_WIDTH]
    wv = w_in[:, 2 * QK_WIDTH:2 * QK_WIDTH + V_WIDTH]
    wf = w_in[:, 2 * QK_WIDTH + V_WIDTH:]
    wqt = jnp.concatenate([wq, _rot_cols(wq)], axis=1).T.astype(bf16)
    wkk = jnp.concatenate([wk, _rot_cols(wk)], axis=1).astype(bf16)
    wvt = wv.T.astype(bf16)
    pad = LANES - N_EXPERTS
    wr = jnp.pad(w_router, ((0, 0), (0, pad)))
    wrh = wr.astype(bf16)
    wrl = (wr - wrh.astype(f32)).astype(bf16)
    br = jnp.pad(b_router, (0, pad), constant_values=NEG_BIG).reshape(1, LANES)
    return dict(wqt=wqt, wk=wkk, wvt=wvt, wf=wf.astype(bf16), wfn=w_fnet.astype(bf16),
                wo=w_o.astype(bf16), wmq=w_mq.astype(bf16), wmkv=w_mkv.astype(bf16),
                wmo=w_mo.astype(bf16), wrh=wrh, wrl=wrl, br=br,
                wgu=w_gu.astype(bf16), wd=w_down.astype(bf16))


def _trunk(x, mem, W, tables, lam_params, g_col, ln, b_gu, b_down):
    B, S, D = x.shape
    cosk, sink, cost, sint = tables["rope"]
    qz, kk, vt5, f = _inproj(x, W["wqt"], W["wk"], W["wvt"], W["wf"], cosk, sink, cost, sint)
    o = _attention(qz, kk, vt5, lam_params, g_col)

    n1 = S // FN2
    w1, mtab, cc, sc = tables["dft"]
    a = _fnet1(f.reshape(B, n1, FN2 * F_WIDTH), w1)
    fo = _fnet2(a.reshape(B, 2, n1, FN2, F_WIDTH), mtab, cc, sc, W["wfn"]).reshape(B, S, F_WIDTH)

    km, vm = _memkv(mem, W["wmkv"])
    x2, x2t, idx, gates, rank, counts = _post(x, o, fo, W["wo"], ln["g1"], ln["b1"], W["wmq"], km, vm,
                                         W["wmo"], ln["g2"], ln["b2"], W["wrh"], W["wrl"], W["br"])

    T = B * S
    N = T * TOP_K
    bm = EXPERT_BM
    counts = counts[0, :N_EXPERTS]
    padded = (counts + bm - 1) // bm * bm
    pstarts = jnp.cumsum(padded) - padded
    R = N + N_EXPERTS * bm
    nb = R // bm
    ends = jnp.cumsum(padded)
    block_e = jnp.minimum(
        jnp.sum((ends[None, :] <= (jnp.arange(nb, dtype=jnp.int32) * bm)[:, None]).astype(jnp.int32), axis=1),
        N_EXPERTS - 1).astype(jnp.int32)
    top_i = idx.reshape(T, LANES)[:, :TOP_K]
    onehot = top_i[:, :, None] == jnp.arange(N_EXPERTS, dtype=jnp.int32)[None, None, :]
    dest = jnp.sum(jnp.where(onehot, pstarts[None, None, :], 0), axis=-1) + rank.reshape(T, LANES)[:, :TOP_K]
    dest3 = dest.astype(jnp.int32).reshape(T // TT, 1, TT * TOP_K)

    last_blk = (nb - 1) * bm
    seg_last = jnp.where(padded > 0, pstarts + padded - bm, last_blk)
    tail = jnp.minimum(ends[-1] + jnp.arange(N_EXPERTS, dtype=jnp.int32) * bm, last_blk)
    zrows = jnp.concatenate([seg_last, tail]).astype(jnp.int32)

    x2f = x2.reshape(T, D)
    xs = _dispatch(zrows, dest3, x2t.reshape(T * SUBLANES, LANES), R)
    ys = _experts(block_e, xs, W["wgu"], b_gu, W["wd"], b_down)
    out = _combine(dest3, ys, x2f, gates.reshape(T, LANES), ln["g3"], ln["b3"])
    return out.reshape(B, S, D)


def kernel(x_prompt, x_sample, mem_prompt, mem_sample, w_in, lambda_q1, lambda_k1, lambda_q2, lambda_k2,
           subln_g, w_fnet, w_o, ln1_g, ln1_b, w_mq, w_mkv, w_mo, ln2_g, ln2_b, w_router, b_router,
           w_gu, b_gu, w_down, b_down, ln3_g, ln3_b):
    l = 0
    W = _prep_weights(w_in[l], w_fnet[l], w_o[l], w_mq[l], w_mkv[l], w_mo[l], w_router[l], b_router[l],
                      w_gu[l], w_down[l])
    lam_params = jnp.stack([lambda_q1[l], lambda_k1[l], lambda_q2[l], lambda_k2[l]], axis=0)
    g_col = subln_g[l].reshape(DIFF_VDIM, 1)
    row = lambda v: v[l].reshape(1, -1)
    ln = dict(g1=row(ln1_g), b1=row(ln1_b), g2=row(ln2_g), b2=row(ln2_b), g3=row(ln3_g), b3=row(ln3_b))
    bgu = b_gu[l].reshape(N_EXPERTS, 1, 2 * D_FF)
    bdn = b_down[l].reshape(N_EXPERTS, 1, D_MODEL)
    tables = {}
    outs = []
    for x, mem in ((x_prompt, mem_prompt), (x_sample, mem_sample)):
        S = x.shape[1]
        if S not in tables:
            tables[S] = dict(rope=_rope_tables(S), dft=_dft_tables(S))
        outs.append(_trunk(x, mem, W, tables[S], lam_params, g_col, ln, bgu, bdn))
    return tuple(outs)
```

```python
import functools
import math

import jax
import jax.numpy as jnp
import numpy as np
from jax import lax
from jax.experimental import pallas as pl
from jax.experimental.pallas import tpu as pltpu

D_MODEL = 1024
N_MEM = 256
MEM_HEADS = 4
MEM_DH = D_MODEL // MEM_HEADS
DIFF_HEADS = 4
DIFF_DH = 64
DIFF_VDIM = 2 * DIFF_DH
QK_WIDTH = DIFF_HEADS * 2 * DIFF_DH
V_WIDTH = DIFF_HEADS * DIFF_VDIM
FNET_GROUPS = 4
FNET_CH = 128
F_WIDTH = FNET_GROUPS * FNET_CH
ROPE_THETA = 10000.0
N_EXPERTS = 32
TOP_K = 4
D_FF = D_MODEL
SWIGLU_LIMIT = 7.0
SWIGLU_ALPHA = 1.702
DEPTH = 1
DEEPNORM_ALPHA = (2.0 * DEPTH) ** 0.25
LN_EPS = 1e-5
SUBLN_EPS = 1e-5
LAMBDA_INIT = 0.8 - 0.6 * math.exp(-0.3 * 0)
LOG2_E = math.log2(math.e)
BF16_SUBLANES = 16
V_ROWS = DIFF_VDIM + BF16_SUBLANES

LANES = 128
SUBLANES = 8
VMEM_LIMIT = 56 * 1024 * 1024

TS_IN = 512
TQ = 256
ATTN_PV_LAG = 3
ATTN_UNROLL = 12
FN2 = 128
F1_COLS = 4096
F2_K1 = 8
POST_TT = 1024
POST_SUB = 256
TT = 256
EXPERT_BM = 512
EXPERT_SUB = 256
DMA_UNROLL = 4
NEG_BIG = -1e30

bf16 = jnp.bfloat16
f32 = jnp.float32


def _cparams(sem):
    return pltpu.CompilerParams(dimension_semantics=sem, vmem_limit_bytes=VMEM_LIMIT)


def _dot(a, b):
    return jnp.dot(a, b, preferred_element_type=f32)


def _dot_nt(a, b):
    return lax.dot_general(a, b, (((1,), (1,)), ((), ())), preferred_element_type=f32)


def _load_token_tiles(ref, lead, n):
    return jnp.concatenate([ref[lead + (pl.ds(j, n, stride=SUBLANES),)] for j in range(D_MODEL // LANES)], axis=-1)


def _store_token_tiles(ref, lead, row0, v):
    for j in range(D_MODEL // LANES):
        ref[lead + (pl.ds(row0 * SUBLANES + j, v.shape[0], stride=SUBLANES),)] = v[:, j * LANES:(j + 1) * LANES]


def _layer_norm(v, g, b):
    mu = jnp.mean(v, axis=-1, keepdims=True)
    var = jnp.mean(jnp.square(v - mu), axis=-1, keepdims=True)
    return (v - mu) * lax.rsqrt(var + LN_EPS) * g + b


def _inproj_kernel(x_ref, wqt_ref, wk_ref, wvt_ref, wf_ref, cosk_ref, sink_ref, cost_ref, sint_ref,
                   qz_ref, kk_ref, vt_ref, f_ref):
    xb = x_ref[0].astype(bf16)
    half = DIFF_DH // 2
    hk = _dot(xb, wk_ref[...])
    ck = cosk_ref[...]
    sk = sink_ref[...]
    first_half = lax.broadcasted_iota(jnp.int32, (xb.shape[0], LANES), 1) % DIFF_DH < half
    for h in range(DIFF_HEADS):
        a = hk[:, h * LANES:(h + 1) * LANES]
        r = jnp.where(first_half, -pltpu.roll(a, LANES - half, axis=1), pltpu.roll(a, half, axis=1))
        kk_ref[0, h] = (a * ck + r * sk).astype(bf16)
    hq = _dot_nt(wqt_ref[...], xb)
    ct = cost_ref[...]
    st = sint_ref[...]
    scale = DIFF_DH ** -0.5 * LOG2_E
    zeros = jnp.zeros((DIFF_DH, xb.shape[0]), bf16)
    for h in range(DIFF_HEADS):
        for m in range(2):
            c = h * 2 + m
            a = hq[c * DIFF_DH:(c + 1) * DIFF_DH]
            r = jnp.concatenate([-a[half:], a[:half]], axis=0)
            q = ((a * ct + r * st) * scale).astype(bf16)
            if m == 0:
                qz_ref[0, h, 0, :DIFF_DH, :] = q
                qz_ref[0, h, 0, DIFF_DH:, :] = zeros
            else:
                qz_ref[0, h, 1, :DIFF_DH, :] = zeros
                qz_ref[0, h, 1, DIFF_DH:, :] = q
    hv = _dot_nt(wvt_ref[...], xb)
    sub = lax.broadcasted_iota(jnp.int32, (V_ROWS - DIFF_VDIM, xb.shape[0]), 0)
    ones_rows = jnp.where(sub == 0, 1.0, 0.0).astype(bf16)
    for h in range(DIFF_HEADS):
        vt_ref[0, h, 0, :DIFF_VDIM, :] = hv[h * DIFF_VDIM:(h + 1) * DIFF_VDIM].astype(bf16)
        vt_ref[0, h, 0, DIFF_VDIM:, :] = ones_rows
    f_ref[0] = _dot(xb, wf_ref[...]).astype(bf16)


def _inproj(x, wqt, wk, wvt, wf, cosk, sink, cost, sint):
    B, S, D = x.shape
    ts = TS_IN
    nchunk = S // ts
    const = lambda shape: pl.BlockSpec(shape, lambda b, i: (0,) * len(shape))
    return pl.pallas_call(
        _inproj_kernel,
        grid=(B, nchunk),
        in_specs=[
            pl.BlockSpec((1, ts, D), lambda b, i: (b, i, 0)),
            const(wqt.shape), const(wk.shape), const(wvt.shape), const(wf.shape),
            pl.BlockSpec((ts, LANES), lambda b, i: (i, 0)),
            pl.BlockSpec((ts, LANES), lambda b, i: (i, 0)),
            pl.BlockSpec((DIFF_DH, ts), lambda b, i: (0, i)),
            pl.BlockSpec((DIFF_DH, ts), lambda b, i: (0, i)),
        ],
        out_specs=[
            pl.BlockSpec((1, DIFF_HEADS, 2, LANES, ts), lambda b, i: (b, 0, 0, 0, i)),
            pl.BlockSpec((1, DIFF_HEADS, ts, LANES), lambda b, i: (b, 0, i, 0)),
            pl.BlockSpec((1, DIFF_HEADS, 1, V_ROWS, ts), lambda b, i: (b, 0, i, 0, 0)),
            pl.BlockSpec((1, ts, F_WIDTH), lambda b, i: (b, i, 0)),
        ],
        out_shape=[
            jax.ShapeDtypeStruct((B, DIFF_HEADS, 2, LANES, S), bf16),
            jax.ShapeDtypeStruct((B, DIFF_HEADS, S, LANES), bf16),
            jax.ShapeDtypeStruct((B, DIFF_HEADS, nchunk, V_ROWS, ts), bf16),
            jax.ShapeDtypeStruct((B, S, F_WIDTH), bf16),
        ],
        compiler_params=_cparams(("parallel", "parallel")),
        name="inproj",
    )(x, wqt, wk, wvt, wf, cosk, sink, cost, sint)


def _attn_kernel(qz_ref, kk_ref, vt_ref, lam_ref, g_ref, o_ref, s_scr, p_scr, acc_ref):
    nchunk = vt_ref.shape[2]
    tk = vt_ref.shape[4]
    tq = qz_ref.shape[4]
    acc_ref[...] = jnp.zeros_like(acc_ref)

    def scores(j, slot):
        kt = kk_ref[0, 0, pl.ds(pl.multiple_of(j * tk, tk), tk), :]
        cms = []
        for mp in range(2):
            s = _dot(kt, qz_ref[0, 0, mp])
            s_scr[slot, mp] = s
            cms.append(jnp.max(s, axis=0, keepdims=True))
        return tuple(cms)

    def probs(s_slot, p_slot, cms, ms):
        m_new, alphas = [], []
        for mp in range(2):
            m = jnp.maximum(ms[mp], cms[mp])
            alphas.append(jnp.exp2(ms[mp] - m))
            p_scr[p_slot, mp] = jnp.exp2(s_scr[s_slot, mp] - m).astype(bf16)
            m_new.append(m)
        return tuple(m_new), tuple(alphas)

    def accumulate(j, slot, alphas):
        vt = vt_ref[0, 0, j]
        for mp in range(2):
            acc_ref[mp] = acc_ref[mp] * alphas[mp] + _dot(vt, p_scr[slot, mp])

    lag = p_scr.shape[0]
    unroll = ATTN_UNROLL
    assert lag >= 2 and unroll % 2 == 0 and unroll % lag == 0 and nchunk >= lag

    def tick(t, r, ms, cm_prev, als):
        peeled = isinstance(t, int)
        cm = cm_prev
        if not peeled or t < nchunk:
            cm = scores(t, r % 2)
        if not peeled or 1 <= t <= nchunk:
            ms, al_new = probs((r - 1) % 2, (r - 1) % lag, cm_prev, ms)
            als = als + (al_new,)
        if not peeled or t >= lag:
            accumulate(t - lag, (r - lag) % lag, als[0])
            als = als[1:]
        return ms, cm, als

    first = lag + (nchunk - lag) % unroll
    ms = (jnp.full((1, tq), NEG_BIG, f32),) * 2
    cm, als = None, ()
    for t in range(first):
        ms, cm, als = tick(t, t % unroll, ms, cm, als)

    def ticks(it, carry):
        ms, cm, als = carry
        t0 = first + it * unroll
        for u in range(unroll):
            ms, cm, als = tick(t0 + u, (first + u) % unroll, ms, cm, als)
        return ms, cm, als

    ms, cm, als = lax.fori_loop(0, (nchunk - first) // unroll, ticks, (ms, cm, als))
    for t in range(nchunk, nchunk + lag):
        ms, cm, als = tick(t, t % unroll, ms, cm, als)

    lp = lam_ref[...]
    lam = (jnp.exp(jnp.sum(lp[0:1] * lp[1:2], axis=1, keepdims=True))
           - jnp.exp(jnp.sum(lp[2:3] * lp[3:4], axis=1, keepdims=True)) + LAMBDA_INIT)
    a0 = acc_ref[0]
    a1 = acc_ref[1]
    o = (a0[:DIFF_VDIM] / a0[DIFF_VDIM:DIFF_VDIM + 1]
         - lam * (a1[:DIFF_VDIM] / a1[DIFF_VDIM:DIFF_VDIM + 1]))
    ms2 = jnp.mean(jnp.square(o), axis=0, keepdims=True)
    y = o * lax.rsqrt(ms2 + SUBLN_EPS) * g_ref[...] * (1.0 - LAMBDA_INIT)
    o_ref[0] = y.T.astype(bf16)


def _attention(qz, kk, vt5, lam_params, g_col):
    B, H, _, _, S = qz.shape
    nchunk, tk = vt5.shape[2], vt5.shape[4]
    return pl.pallas_call(
        _attn_kernel,
        grid=(B, H, S // TQ),
        in_specs=[
            pl.BlockSpec((1, 1, 2, LANES, TQ), lambda b, h, i: (b, h, 0, 0, i)),
            pl.BlockSpec((1, 1, S, LANES), lambda b, h, i: (b, h, 0, 0)),
            pl.BlockSpec((1, 1, nchunk, V_ROWS, tk), lambda b, h, i: (b, h, 0, 0, 0)),
            pl.BlockSpec(lam_params.shape, lambda b, h, i: (0, 0)),
            pl.BlockSpec(g_col.shape, lambda b, h, i: (0, 0)),
        ],
        out_specs=pl.BlockSpec((1, TQ, DIFF_VDIM), lambda b, h, i: (b, i, h)),
        out_shape=jax.ShapeDtypeStruct((B, S, V_WIDTH), bf16),
        scratch_shapes=[pltpu.VMEM((2, 2, tk, TQ), f32),
                        pltpu.VMEM((ATTN_PV_LAG, 2, tk, TQ), bf16),
                        pltpu.VMEM((2, V_ROWS, TQ), f32)],
        compiler_params=_cparams(("parallel", "parallel", "parallel")),
        name="diffattn",
    )(qz, kk, vt5, lam_params, g_col)


def _fnet1_kernel(w1_ref, f_ref, a_ref):
    a_ref[0] = _dot(w1_ref[...], f_ref[0]).astype(bf16)


def _fnet1(f2d, w1):
    B, n1, cols = f2d.shape
    return pl.pallas_call(
        _fnet1_kernel,
        grid=(B, cols // F1_COLS),
        in_specs=[pl.BlockSpec(w1.shape, lambda b, j: (0, 0)),
                  pl.BlockSpec((1, n1, F1_COLS), lambda b, j: (b, 0, j))],
        out_specs=pl.BlockSpec((1, 2 * n1, F1_COLS), lambda b, j: (b, 0, j)),
        out_shape=jax.ShapeDtypeStruct((B, 2 * n1, cols), bf16),
        compiler_params=_cparams(("parallel", "parallel")),
        name="fnet_stage1",
    )(w1, f2d)


def _fnet2_kernel(m_ref, a_ref, cc_ref, sc_ref, wf_ref, o_ref):
    n2 = a_ref.shape[3]
    nk = a_ref.shape[2]
    groups = [slice(g * FNET_CH, (g + 1) * FNET_CH) for g in range(FNET_GROUPS)]
    bris = [_dot(m_ref[kk], jnp.concatenate([a_ref[0, 0, kk], a_ref[0, 1, kk]], axis=0))
            for kk in range(nk)]
    brs = [b[:n2].astype(bf16) for b in bris]
    bis = [b[n2:].astype(bf16) for b in bris]
    frs = [[(_dot(brs[kk][:, sl], cc_ref[...]) + _dot(bis[kk][:, sl], sc_ref[...])).astype(bf16) for sl in groups]
           for kk in range(nk)]
    for kk in range(nk):
        for g, sl in enumerate(groups):
            fo = _dot(frs[kk][g], wf_ref[g])
            o_ref[0, :, kk * F_WIDTH + g * FNET_CH:kk * F_WIDTH + (g + 1) * FNET_CH] = fo.astype(bf16)


def _fnet2(a5, mtab, cc, sc, wfn):
    B, _, n1, n2, c = a5.shape
    return pl.pallas_call(
        _fnet2_kernel,
        grid=(B, n1 // F2_K1),
        in_specs=[pl.BlockSpec((F2_K1, 2 * n2, 2 * n2), lambda b, j: (j, 0, 0)),
                  pl.BlockSpec((1, 2, F2_K1, n2, c), lambda b, j: (b, 0, j, 0, 0)),
                  pl.BlockSpec(cc.shape, lambda b, j: (0, 0)),
                  pl.BlockSpec(sc.shape, lambda b, j: (0, 0)),
                  pl.BlockSpec(wfn.shape, lambda b, j: (0, 0, 0))],
        out_specs=pl.BlockSpec((1, n2, F2_K1 * c), lambda b, j: (b, 0, j)),
        out_shape=jax.ShapeDtypeStruct((B, n2, n1 * c), bf16),
        compiler_params=_cparams(("parallel", "parallel")),
        name="fnet_stage2",
    )(mtab, a5, cc, sc, wfn)


def _dft_tables(S):
    n1, n2 = S // FN2, FN2
    i1 = jnp.arange(n1, dtype=jnp.int32)
    ang1 = (2.0 * math.pi / n1) * ((i1[:, None] * i1[None, :]) % n1).astype(f32)
    w1 = jnp.concatenate([jnp.cos(ang1), -jnp.sin(ang1)], axis=0) * (n1 ** -0.5)
    k = i1[:, None, None] + n1 * jnp.arange(n2, dtype=jnp.int32)[None, :, None]
    nn = jnp.arange(n2, dtype=jnp.int32)[None, None, :]
    ang2 = (2.0 * math.pi / S) * ((k * nn) % S).astype(f32)
    c2 = jnp.cos(ang2) * (n2 ** -0.5)
    s2 = jnp.sin(ang2) * (n2 ** -0.5)
    mtab = jnp.concatenate([jnp.concatenate([c2, s2], axis=2),
                            jnp.concatenate([-s2, c2], axis=2)], axis=1)
    ic = jnp.arange(FNET_CH, dtype=jnp.int32)
    angc = (2.0 * math.pi / FNET_CH) * ((ic[:, None] * ic[None, :]) % FNET_CH).astype(f32)
    cc = jnp.cos(angc) * (FNET_CH ** -0.5)
    sc = jnp.sin(angc) * (FNET_CH ** -0.5)
    return w1.astype(bf16), mtab.astype(bf16), cc.astype(bf16), sc.astype(bf16)


def _memkv_kernel(mem_ref, w_ref, k_ref, v_ref):
    kv = _dot(mem_ref[0].astype(bf16), w_ref[...])
    k_ref[0] = kv[:, :D_MODEL].astype(bf16)
    v_ref[0] = kv[:, D_MODEL:].astype(bf16)


def _memkv(mem, w_mkv):
    B, M, D = mem.shape
    return pl.pallas_call(
        _memkv_kernel,
        grid=(B,),
        in_specs=[pl.BlockSpec((1, M, D), lambda b: (b, 0, 0)),
                  pl.BlockSpec(w_mkv.shape, lambda b: (0, 0))],
        out_specs=[pl.BlockSpec((1, M, D), lambda b: (b, 0, 0))] * 2,
        out_shape=[jax.ShapeDtypeStruct((B, M, D), bf16)] * 2,
        compiler_params=_cparams(("parallel",)),
        name="memkv",
    )(mem, w_mkv)


def _post_kernel(x_ref, o_ref, fo_ref, wo_ref, g1_ref, b1_ref, wmq_ref, km_ref, vm_ref, wmo_ref,
                 g2_ref, b2_ref, wrh_ref, wrl_ref, br_ref,
                 x2_ref, x2t_ref, idx_ref, gate_ref, rank_ref, cnt_ref, carry_ref):
    first = jnp.logical_and(pl.program_id(0) == 0, pl.program_id(1) == 0)

    @pl.when(first)
    def _():
        carry_ref[...] = jnp.zeros_like(carry_ref)

    n_sub = x_ref.shape[1] // POST_SUB
    gens = [_post_rows(slice(s * POST_SUB, (s + 1) * POST_SUB), x_ref, o_ref, fo_ref, wo_ref, g1_ref, b1_ref,
                       wmq_ref, km_ref, vm_ref, wmo_ref, g2_ref, b2_ref, wrh_ref, wrl_ref, br_ref, x2_ref, x2t_ref)
            for s in range(n_sub)]
    routed = [None] * n_sub
    active = list(range(n_sub))
    while active:
        for s in list(active):
            try:
                next(gens[s])
            except StopIteration as done:
                routed[s] = done.value
                active.remove(s)

    carry = carry_ref[...]
    for s in range(n_sub):
        rows = slice(s * POST_SUB, (s + 1) * POST_SUB)
        lane, idxs, gates, before, sel_sum = routed[s]
        idx_out = jnp.zeros((POST_SUB, LANES), f32)
        gate_out = jnp.zeros((POST_SUB, LANES), f32)
        rank_out = jnp.zeros((POST_SUB, LANES), f32)
        for k in range(TOP_K):
            rk = jnp.sum(jnp.where(lane == idxs[k], before + carry, 0.0), axis=-1, keepdims=True)
            idx_out = jnp.where(lane == float(k), idxs[k], idx_out)
            gate_out = jnp.where(lane == float(k), gates[k], gate_out)
            rank_out = jnp.where(lane == float(k), rk, rank_out)
        idx_ref[0, rows] = idx_out.astype(jnp.int32)
        gate_ref[0, rows] = gate_out
        rank_ref[0, rows] = rank_out.astype(jnp.int32)
        carry = carry + sel_sum
    carry_ref[...] = carry
    cnt_ref[...] = carry.astype(jnp.int32)


def _post_rows(rows, x_ref, o_ref, fo_ref, wo_ref, g1_ref, b1_ref, wmq_ref, km_ref, vm_ref, wmo_ref,
               g2_ref, b2_ref, wrh_ref, wrl_ref, br_ref, x2_ref, x2t_ref):
    x = x_ref[0, rows]
    tt = x.shape[0]
    h = _dot(o_ref[0, rows], wo_ref[:V_WIDTH, :]) + _dot(fo_ref[0, rows], wo_ref[V_WIDTH:, :])
    yield
    x1 = _layer_norm(DEEPNORM_ALPHA * x + h, g1_ref[...], b1_ref[...])
    qm = _dot(x1.astype(bf16), wmq_ref[...]).astype(bf16)
    yield
    heads = []
    for hh in range(MEM_HEADS):
        sl = slice(hh * MEM_DH, (hh + 1) * MEM_DH)
        s = _dot_nt(qm[:, sl], km_ref[0, :, sl]) * (MEM_DH ** -0.5)
        s = s - jnp.max(s, axis=-1, keepdims=True)
        e = jnp.exp(s)
        p = e / jnp.sum(e, axis=-1, keepdims=True)
        heads.append(_dot(p.astype(bf16), vm_ref[0, :, sl]))
        yield
    om = jnp.concatenate(heads, axis=-1).astype(bf16)
    h2 = _dot(om, wmo_ref[...])
    yield
    x2 = _layer_norm(DEEPNORM_ALPHA * x1 + h2, g2_ref[...], b2_ref[...])
    x2_ref[0, rows] = x2
    _store_token_tiles(x2t_ref, (0,), rows.start, x2)

    xh = x2.astype(bf16)
    xl = (x2 - xh.astype(f32)).astype(bf16)
    logits = (_dot(xh, wrh_ref[...]) + _dot(xl, wrh_ref[...]) + _dot(xh, wrl_ref[...])) + br_ref[...]
    yield

    lane = lax.broadcasted_iota(jnp.int32, (tt, LANES), 1).astype(f32)
    work = logits
    vals, idxs = [], []
    for _ in range(TOP_K):
        mx = jnp.max(work, axis=-1, keepdims=True)
        ix = jnp.min(jnp.where(work == mx, lane, float(LANES)), axis=-1, keepdims=True)
        vals.append(mx)
        idxs.append(ix)
        work = jnp.where(lane == ix, -jnp.inf, work)
    es = [jnp.exp(v - vals[0]) for v in vals]
    den = es[0] + es[1] + es[2] + es[3]
    gates = [e / den for e in es]
    yield

    sel = jnp.zeros((tt, LANES), f32)
    for ix in idxs:
        sel = sel + jnp.where(lane == ix, 1.0, 0.0)
    row = lax.broadcasted_iota(jnp.int32, (tt, tt), 0)
    col = lax.broadcasted_iota(jnp.int32, (tt, tt), 1)
    ltri = jnp.where(col < row, 1.0, 0.0).astype(bf16)
    before = _dot(ltri, sel.astype(bf16))
    return lane, idxs, gates, before, jnp.sum(sel, axis=0, keepdims=True)


def _post(x, o, fo, wo, g1, b1, wmq, km, vm, wmo, g2, b2, wrh, wrl, br):
    B, S, D = x.shape
    tok = lambda w: pl.BlockSpec((1, POST_TT, w), lambda b, i: (b, i, 0))
    const = lambda a: pl.BlockSpec(a.shape, lambda b, i: (0,) * a.ndim)
    mem = lambda a: pl.BlockSpec((1,) + a.shape[1:], lambda b, i: (b, 0, 0))
    return pl.pallas_call(
        _post_kernel,
        grid=(B, S // POST_TT),
        in_specs=[tok(D), tok(V_WIDTH), tok(F_WIDTH), const(wo), const(g1), const(b1), const(wmq),
                  mem(km), mem(vm), const(wmo), const(g2), const(b2), const(wrh), const(wrl), const(br)],
        out_specs=[tok(D), pl.BlockSpec((1, POST_TT * SUBLANES, LANES), lambda b, i: (b, i, 0)),
                   tok(LANES), tok(LANES), tok(LANES),
                   pl.BlockSpec((1, LANES), lambda b, i: (0, 0))],
        out_shape=[jax.ShapeDtypeStruct((B, S, D), f32),
                   jax.ShapeDtypeStruct((B, S * SUBLANES, LANES), f32),
                   jax.ShapeDtypeStruct((B, S, LANES), jnp.int32),
                   jax.ShapeDtypeStruct((B, S, LANES), f32),
                   jax.ShapeDtypeStruct((B, S, LANES), jnp.int32),
                   jax.ShapeDtypeStruct((1, LANES), jnp.int32)],
        scratch_shapes=[pltpu.VMEM((1, LANES), f32)],
        compiler_params=_cparams(("arbitrary", "arbitrary")),
        name="post_mixer_router",
    )(x, o, fo, wo, g1, b1, wmq, km, vm, wmo, g2, b2, wrh, wrl, br)


def _tile_rows(t):
    return pl.ds(pl.multiple_of(t * SUBLANES, SUBLANES), SUBLANES)


def _row_copy_out(x_ref, xs_ref, sem, t, d):
    return pltpu.make_async_copy(x_ref.at[_tile_rows(t)], xs_ref.at[_tile_rows(d)], sem)


def _zero_copy(zero_ref, xs_ref, sem, row):
    return pltpu.make_async_copy(
        zero_ref, xs_ref.at[pl.ds(pl.multiple_of(row * SUBLANES, SUBLANES), zero_ref.shape[0])], sem)


def _dispatch_kernel(zrow_ref, dest_ref, x_ref, xs_ref, zero_ref, zsem, sem):
    i = pl.program_id(0)
    rows_per_tile = TT * TOP_K

    @pl.when(i == 0)
    def _():
        zero_ref[...] = jnp.zeros_like(zero_ref)
        for j in range(zrow_ref.shape[0]):
            _zero_copy(zero_ref, xs_ref, zsem, zrow_ref[j]).start()
        for j in range(zrow_ref.shape[0]):
            _zero_copy(zero_ref, xs_ref, zsem, 0).wait()

    def issue(g, c):
        r0 = g * DMA_UNROLL
        dests = [dest_ref[0, 0, r0 * TOP_K + j] for j in range(DMA_UNROLL * TOP_K)]
        for u in range(DMA_UNROLL):
            for k in range(TOP_K):
                _row_copy_out(x_ref, xs_ref, sem, r0 + u, dests[u * TOP_K + k]).start(priority=k % 2)
        return c

    lax.fori_loop(0, TT // DMA_UNROLL, issue, 0)

    def drain(g, c):
        for u in range(DMA_UNROLL * TOP_K):
            _row_copy_out(x_ref, xs_ref, sem, 0, 0).wait()
        return c

    lax.fori_loop(0, rows_per_tile // (DMA_UNROLL * TOP_K), drain, 0)


def _dispatch(zrows, dest3, x2_tiles, R):
    T = x2_tiles.shape[0] // SUBLANES
    grid_spec = pltpu.PrefetchScalarGridSpec(
        num_scalar_prefetch=1,
        grid=(T // TT,),
        in_specs=[pl.BlockSpec((1, 1, TT * TOP_K), lambda i, z: (i, 0, 0), memory_space=pltpu.SMEM),
                  pl.BlockSpec((TT * SUBLANES, LANES), lambda i, z: (i, 0))],
        out_specs=pl.BlockSpec(memory_space=pl.ANY),
        scratch_shapes=[pltpu.VMEM((EXPERT_BM * SUBLANES, LANES), f32),
                        pltpu.SemaphoreType.DMA, pltpu.SemaphoreType.DMA],
    )
    return pl.pallas_call(
        _dispatch_kernel,
        grid_spec=grid_spec,
        out_shape=jax.ShapeDtypeStruct((R * SUBLANES, LANES), f32),
        compiler_params=_cparams(("arbitrary",)),
        name="moe_dispatch",
    )(zrows, dest3, x2_tiles)


def _expert_kernel(be_ref, nused_ref, xs_ref, wgu_ref, bgu_ref, wd_ref, bd_ref, ys_ref):
    del be_ref
    in_use = pl.program_id(0) < nused_ref[0]

    @pl.when(in_use)
    def _():
        _expert_block(xs_ref, wgu_ref, bgu_ref, wd_ref, bd_ref, ys_ref)

    @pl.when(jnp.logical_not(in_use))
    def _():
        ys_ref[...] = jnp.zeros_like(ys_ref)


def _expert_block(xs_ref, wgu_ref, bgu_ref, wd_ref, bd_ref, ys_ref):
    bm = xs_ref.shape[0] // SUBLANES
    n_sub = bm // EXPERT_SUB
    xs = [jnp.concatenate([xs_ref[pl.ds(s * EXPERT_SUB * SUBLANES + j, EXPERT_SUB, stride=SUBLANES)]
                           for j in range(D_MODEL // LANES)], axis=-1).astype(bf16) for s in range(n_sub)]
    hbs = [_dot(x, wgu_ref[0]) + bgu_ref[0] for x in xs]
    acts = []
    for hb in hbs:
        g = jnp.minimum(hb[:, :D_FF], SWIGLU_LIMIT)
        u = jnp.clip(hb[:, D_FF:], -SWIGLU_LIMIT, SWIGLU_LIMIT)
        acts.append(((u + 1.0) * (g * jax.nn.sigmoid(g * SWIGLU_ALPHA))).astype(bf16))
    for s, a in enumerate(acts):
        _store_token_tiles(ys_ref, (), s * EXPERT_SUB, _dot(a, wd_ref[0]) + bd_ref[0])


def _experts(block_e, n_used, xs, wgu, bgu, wd, bd):
    D = D_MODEL
    rows = EXPERT_BM * SUBLANES
    grid_spec = pltpu.PrefetchScalarGridSpec(
        num_scalar_prefetch=2,
        grid=(xs.shape[0] // rows,),
        in_specs=[pl.BlockSpec((rows, LANES), lambda i, be, nu: (i, 0)),
                  pl.BlockSpec((1, D, 2 * D_FF), lambda i, be, nu: (be[i], 0, 0)),
                  pl.BlockSpec((1, 1, 2 * D_FF), lambda i, be, nu: (be[i], 0, 0)),
                  pl.BlockSpec((1, D_FF, D), lambda i, be, nu: (be[i], 0, 0)),
                  pl.BlockSpec((1, 1, D), lambda i, be, nu: (be[i], 0, 0))],
        out_specs=pl.BlockSpec((rows, LANES), lambda i, be, nu: (i, 0)),
    )
    return pl.pallas_call(
        _expert_kernel,
        grid_spec=grid_spec,
        out_shape=jax.ShapeDtypeStruct(xs.shape, f32),
        compiler_params=_cparams(("arbitrary",)),
        name="moe_experts",
    )(block_e, n_used, xs, wgu, bgu, wd, bd)


def _row_copy_in(ys_ref, buf_ref, sems, slot, k, r, d):
    return pltpu.make_async_copy(ys_ref.at[_tile_rows(d)], buf_ref.at[slot, k, _tile_rows(r)], sems.at[slot])


def _combine_kernel(dest_ref, dest_next_ref, ys_ref, x2_ref, gate_ref, g3_ref, b3_ref, out_ref, buf_ref, sems):
    i = pl.program_id(0)
    slot = i % 2

    def start_tile(d_ref, s):
        def issue(g, c):
            r0 = g * DMA_UNROLL
            srcs = [d_ref[0, 0, r0 * TOP_K + j] for j in range(DMA_UNROLL * TOP_K)]
            for u in range(DMA_UNROLL):
                for k in range(TOP_K):
                    _row_copy_in(ys_ref, buf_ref, sems, s, k, r0 + u, srcs[u * TOP_K + k]).start(priority=k % 2)
            return c

        lax.fori_loop(0, TT // DMA_UNROLL, issue, 0)

    @pl.when(i == 0)
    def _():
        start_tile(dest_ref, 0)

    @pl.when(i + 1 < pl.num_programs(0))
    def _():
        start_tile(dest_next_ref, 1 - slot)

    def drain(g, c):
        for u in range(DMA_UNROLL * TOP_K):
            _row_copy_in(ys_ref, buf_ref, sems, slot, 0, 0, 0).wait()
        return c

    lax.fori_loop(0, TT // DMA_UNROLL, drain, 0)

    gates = gate_ref[...]
    y = gates[:, 0:1] * _load_token_tiles(buf_ref, (slot, 0), TT)
    for k in range(1, TOP_K):
        y = y + gates[:, k:k + 1] * _load_token_tiles(buf_ref, (slot, k), TT)
    out_ref[...] = _layer_norm(DEEPNORM_ALPHA * x2_ref[...] + y, g3_ref[...], b3_ref[...])


def _combine(dest3, ys, x2_flat, gates_flat, g3, b3):
    T, D = x2_flat.shape
    n_tiles = T // TT
    dest_spec = lambda off: pl.BlockSpec((1, 1, TT * TOP_K), lambda i: (jnp.minimum(i + off, n_tiles - 1), 0, 0),
                                         memory_space=pltpu.SMEM)
    return pl.pallas_call(
        _combine_kernel,
        grid=(n_tiles,),
        in_specs=[dest_spec(0), dest_spec(1),
                  pl.BlockSpec(memory_space=pl.ANY),
                  pl.BlockSpec((TT, D), lambda i: (i, 0)),
                  pl.BlockSpec((TT, LANES), lambda i: (i, 0)),
                  pl.BlockSpec(g3.shape, lambda i: (0, 0)),
                  pl.BlockSpec(b3.shape, lambda i: (0, 0))],
        out_specs=pl.BlockSpec((TT, D), lambda i: (i, 0)),
        out_shape=jax.ShapeDtypeStruct((T, D), f32),
        scratch_shapes=[pltpu.VMEM((2, TOP_K, TT * SUBLANES, LANES), f32), pltpu.SemaphoreType.DMA((2,))],
        compiler_params=_cparams(("arbitrary",)),
        name="moe_combine",
    )(dest3, dest3, ys, x2_flat, gates_flat, g3, b3)


def _rope_tables(S):
    half = DIFF_DH // 2
    inv = 1.0 / (ROPE_THETA ** (jnp.arange(half, dtype=f32) / half))
    ang = jnp.arange(S, dtype=f32)[:, None] * inv[None, :]
    cos = jnp.concatenate([jnp.cos(ang), jnp.cos(ang)], -1)
    sin = jnp.concatenate([jnp.sin(ang), jnp.sin(ang)], -1)
    return (jnp.concatenate([cos, cos], -1), jnp.concatenate([sin, sin], -1), cos.T, sin.T)


def _prep_weights(w_in, w_fnet, w_o, w_mq, w_mkv, w_mo, w_router, b_router, w_gu, w_down):
    wq = w_in[:, :QK_WIDTH]
    wk = w_in[:, QK_WIDTH:2 * QK_WIDTH]
    wv = w_in[:, 2 * QK_WIDTH:2 * QK_WIDTH + V_WIDTH]
    wf = w_in[:, 2 * QK_WIDTH + V_WIDTH:]
    wqt = wq.T.astype(bf16)
    wkk = wk.astype(bf16)
    wvt = wv.T.astype(bf16)
    pad = LANES - N_EXPERTS
    wr = jnp.pad(w_router, ((0, 0), (0, pad)))
    wrh = wr.astype(bf16)
    wrl = (wr - wrh.astype(f32)).astype(bf16)
    br = jnp.pad(b_router, (0, pad), constant_values=NEG_BIG).reshape(1, LANES)
    return dict(wqt=wqt, wk=wkk, wvt=wvt, wf=wf.astype(bf16), wfn=w_fnet.astype(bf16),
                wo=w_o.astype(bf16), wmq=w_mq.astype(bf16), wmkv=w_mkv.astype(bf16),
                wmo=w_mo.astype(bf16), wrh=wrh, wrl=wrl, br=br,
                wgu=w_gu.astype(bf16), wd=w_down.astype(bf16))


def _trunk(x, mem, W, tables, lam_params, g_col, ln, b_gu, b_down):
    B, S, D = x.shape
    cosk, sink, cost, sint = tables["rope"]
    qz, kk, vt5, f = _inproj(x, W["wqt"], W["wk"], W["wvt"], W["wf"], cosk, sink, cost, sint)
    o = _attention(qz, kk, vt5, lam_params, g_col)

    n1 = S // FN2
    w1, mtab, cc, sc = tables["dft"]
    a = _fnet1(f.reshape(B, n1, FN2 * F_WIDTH), w1)
    fo = _fnet2(a.reshape(B, 2, n1, FN2, F_WIDTH), mtab, cc, sc, W["wfn"]).reshape(B, S, F_WIDTH)

    km, vm = _memkv(mem, W["wmkv"])
    x2, x2t, idx, gates, rank, counts = _post(x, o, fo, W["wo"], ln["g1"], ln["b1"], W["wmq"], km, vm,
                                         W["wmo"], ln["g2"], ln["b2"], W["wrh"], W["wrl"], W["br"])

    T = B * S
    N = T * TOP_K
    bm = EXPERT_BM
    counts = counts[0, :N_EXPERTS]
    padded = (counts + bm - 1) // bm * bm
    pstarts = jnp.cumsum(padded) - padded
    R = N + N_EXPERTS * bm
    nb = R // bm
    ends = jnp.cumsum(padded)
    block_e = jnp.minimum(
        jnp.sum((ends[None, :] <= (jnp.arange(nb, dtype=jnp.int32) * bm)[:, None]).astype(jnp.int32), axis=1),
        N_EXPERTS - 1).astype(jnp.int32)
    top_i = idx.reshape(T, LANES)[:, :TOP_K]
    onehot = top_i[:, :, None] == jnp.arange(N_EXPERTS, dtype=jnp.int32)[None, None, :]
    dest = jnp.sum(jnp.where(onehot, pstarts[None, None, :], 0), axis=-1) + rank.reshape(T, LANES)[:, :TOP_K]
    dest3 = dest.astype(jnp.int32).reshape(T // TT, 1, TT * TOP_K)

    last_blk = (nb - 1) * bm
    seg_last = jnp.where(padded > 0, pstarts + padded - bm, last_blk)
    tail = jnp.minimum(ends[-1] + jnp.arange(N_EXPERTS, dtype=jnp.int32) * bm, last_blk)
    zrows = jnp.concatenate([seg_last, tail]).astype(jnp.int32)

    x2f = x2.reshape(T, D)
    xs = _dispatch(zrows, dest3, x2t.reshape(T * SUBLANES, LANES), R)
    n_used = (ends[-1] // bm).astype(jnp.int32).reshape(1)
    ys = _experts(block_e, n_used, xs, W["wgu"], b_gu, W["wd"], b_down)
    out = _combine(dest3, ys, x2f, gates.reshape(T, LANES), ln["g3"], ln["b3"])
    return out.reshape(B, S, D)


def kernel(x_prompt, x_sample, mem_prompt, mem_sample, w_in, lambda_q1, lambda_k1, lambda_q2, lambda_k2,
           subln_g, w_fnet, w_o, ln1_g, ln1_b, w_mq, w_mkv, w_mo, ln2_g, ln2_b, w_router, b_router,
           w_gu, b_gu, w_down, b_down, ln3_g, ln3_b):
    l = 0
    W = _prep_weights(w_in[l], w_fnet[l], w_o[l], w_mq[l], w_mkv[l], w_mo[l], w_router[l], b_router[l],
                      w_gu[l], w_down[l])
    lam_params = jnp.stack([lambda_q1[l], lambda_k1[l], lambda_q2[l], lambda_k2[l]], axis=0)
    g_col = subln_g[l].reshape(DIFF_VDIM, 1)
    row = lambda v: v[l].reshape(1, -1)
    ln = dict(g1=row(ln1_g), b1=row(ln1_b), g2=row(ln2_g), b2=row(ln2_b), g3=row(ln3_g), b3=row(ln3_b))
    bgu = b_gu[l].reshape(N_EXPERTS, 1, 2 * D_FF)
    bdn = b_down[l].reshape(N_EXPERTS, 1, D_MODEL)
    tables = {}
    outs = []
    for x, mem in ((x_prompt, mem_prompt), (x_sample, mem_sample)):
        S = x.shape[1]
        if S not in tables:
            tables[S] = dict(rope=_rope_tables(S), dft=_dft_tables(S))
        outs.append(_trunk(x, mem, W, tables[S], lam_params, g_col, ln, bgu, bdn))
    return tuple(outs)
```

```python
import functools
import math

import jax
import jax.numpy as jnp
import numpy as np
from jax import lax
from jax.experimental import pallas as pl
from jax.experimental.pallas import tpu as pltpu

D_MODEL = 1024
N_MEM = 256
MEM_HEADS = 4
MEM_DH = D_MODEL // MEM_HEADS
DIFF_HEADS = 4
DIFF_DH = 64
DIFF_VDIM = 2 * DIFF_DH
QK_WIDTH = DIFF_HEADS * 2 * DIFF_DH
V_WIDTH = DIFF_HEADS * DIFF_VDIM
FNET_GROUPS = 4
FNET_CH = 128
F_WIDTH = FNET_GROUPS * FNET_CH
ROPE_THETA = 10000.0
N_EXPERTS = 32
TOP_K = 4
D_FF = D_MODEL
SWIGLU_LIMIT = 7.0
SWIGLU_ALPHA = 1.702
DEPTH = 1
DEEPNORM_ALPHA = (2.0 * DEPTH) ** 0.25
LN_EPS = 1e-5
SUBLN_EPS = 1e-5
LAMBDA_INIT = 0.8 - 0.6 * math.exp(-0.3 * 0)
LOG2_E = math.log2(math.e)
BF16_SUBLANES = 16
V_ROWS = DIFF_VDIM + BF16_SUBLANES

LANES = 128
SUBLANES = 8
VMEM_LIMIT = 56 * 1024 * 1024

TS_IN = 512
TQ = 256
ATTN_PV_LAG = 3
ATTN_UNROLL = 12
FN2 = 128
F1_COLS = 4096
F2_K1 = 8
POST_TT = 1024
POST_SUB = 256
TT = 512
EXPERT_BM = 512
EXPERT_SUB = 256
DMA_UNROLL = 4
NEG_BIG = -1e30

bf16 = jnp.bfloat16
f32 = jnp.float32


def _cparams(sem):
    return pltpu.CompilerParams(dimension_semantics=sem, vmem_limit_bytes=VMEM_LIMIT)


def _dot(a, b):
    return jnp.dot(a, b, preferred_element_type=f32)


def _dot_nt(a, b):
    return lax.dot_general(a, b, (((1,), (1,)), ((), ())), preferred_element_type=f32)


def _load_token_tiles(ref, lead, n):
    return jnp.concatenate([ref[lead + (pl.ds(j, n, stride=SUBLANES),)] for j in range(D_MODEL // LANES)], axis=-1)


def _store_token_tiles(ref, lead, row0, v):
    for j in range(D_MODEL // LANES):
        ref[lead + (pl.ds(row0 * SUBLANES + j, v.shape[0], stride=SUBLANES),)] = v[:, j * LANES:(j + 1) * LANES]


def _layer_norm(v, g, b):
    mu = jnp.mean(v, axis=-1, keepdims=True)
    var = jnp.mean(jnp.square(v - mu), axis=-1, keepdims=True)
    return (v - mu) * lax.rsqrt(var + LN_EPS) * g + b


def _inproj_kernel(x_ref, wqt_ref, wk_ref, wvt_ref, wf_ref, cosk_ref, sink_ref, cost_ref, sint_ref,
                   qz_ref, kk_ref, vt_ref, f_ref):
    xb = x_ref[0].astype(bf16)
    half = DIFF_DH // 2
    hk = _dot(xb, wk_ref[...])
    ck = cosk_ref[...]
    sk = sink_ref[...]
    first_half = lax.broadcasted_iota(jnp.int32, (xb.shape[0], LANES), 1) % DIFF_DH < half
    for h in range(DIFF_HEADS):
        a = hk[:, h * LANES:(h + 1) * LANES]
        r = jnp.where(first_half, -pltpu.roll(a, LANES - half, axis=1), pltpu.roll(a, half, axis=1))
        kk_ref[0, h] = (a * ck + r * sk).astype(bf16)
    hq = _dot_nt(wqt_ref[...], xb)
    ct = cost_ref[...]
    st = sint_ref[...]
    scale = DIFF_DH ** -0.5 * LOG2_E
    zeros = jnp.zeros((DIFF_DH, xb.shape[0]), bf16)
    for h in range(DIFF_HEADS):
        for m in range(2):
            c = h * 2 + m
            a = hq[c * DIFF_DH:(c + 1) * DIFF_DH]
            r = jnp.concatenate([-a[half:], a[:half]], axis=0)
            q = ((a * ct + r * st) * scale).astype(bf16)
            if m == 0:
                qz_ref[0, h, 0, :DIFF_DH, :] = q
                qz_ref[0, h, 0, DIFF_DH:, :] = zeros
            else:
                qz_ref[0, h, 1, :DIFF_DH, :] = zeros
                qz_ref[0, h, 1, DIFF_DH:, :] = q
    hv = _dot_nt(wvt_ref[...], xb)
    sub = lax.broadcasted_iota(jnp.int32, (V_ROWS - DIFF_VDIM, xb.shape[0]), 0)
    ones_rows = jnp.where(sub == 0, 1.0, 0.0).astype(bf16)
    for h in range(DIFF_HEADS):
        vt_ref[0, h, 0, :DIFF_VDIM, :] = hv[h * DIFF_VDIM:(h + 1) * DIFF_VDIM].astype(bf16)
        vt_ref[0, h, 0, DIFF_VDIM:, :] = ones_rows
    f_ref[0] = _dot(xb, wf_ref[...]).astype(bf16)


def _inproj(x, wqt, wk, wvt, wf, cosk, sink, cost, sint):
    B, S, D = x.shape
    ts = TS_IN
    nchunk = S // ts
    const = lambda shape: pl.BlockSpec(shape, lambda b, i: (0,) * len(shape))
    return pl.pallas_call(
        _inproj_kernel,
        grid=(B, nchunk),
        in_specs=[
            pl.BlockSpec((1, ts, D), lambda b, i: (b, i, 0)),
            const(wqt.shape), const(wk.shape), const(wvt.shape), const(wf.shape),
            pl.BlockSpec((ts, LANES), lambda b, i: (i, 0)),
            pl.BlockSpec((ts, LANES), lambda b, i: (i, 0)),
            pl.BlockSpec((DIFF_DH, ts), lambda b, i: (0, i)),
            pl.BlockSpec((DIFF_DH, ts), lambda b, i: (0, i)),
        ],
        out_specs=[
            pl.BlockSpec((1, DIFF_HEADS, 2, LANES, ts), lambda b, i: (b, 0, 0, 0, i)),
            pl.BlockSpec((1, DIFF_HEADS, ts, LANES), lambda b, i: (b, 0, i, 0)),
            pl.BlockSpec((1, DIFF_HEADS, 1, V_ROWS, ts), lambda b, i: (b, 0, i, 0, 0)),
            pl.BlockSpec((1, ts, F_WIDTH), lambda b, i: (b, i, 0)),
        ],
        out_shape=[
            jax.ShapeDtypeStruct((B, DIFF_HEADS, 2, LANES, S), bf16),
            jax.ShapeDtypeStruct((B, DIFF_HEADS, S, LANES), bf16),
            jax.ShapeDtypeStruct((B, DIFF_HEADS, nchunk, V_ROWS, ts), bf16),
            jax.ShapeDtypeStruct((B, S, F_WIDTH), bf16),
        ],
        compiler_params=_cparams(("parallel", "parallel")),
        name="inproj",
    )(x, wqt, wk, wvt, wf, cosk, sink, cost, sint)


def _attn_kernel(qz_ref, kk_ref, vt_ref, lam_ref, g_ref, o_ref, s_scr, p_scr, acc_ref):
    nchunk = vt_ref.shape[2]
    tk = vt_ref.shape[4]
    tq = qz_ref.shape[4]
    acc_ref[...] = jnp.zeros_like(acc_ref)

    def scores(j, slot):
        kt = kk_ref[0, 0, pl.ds(pl.multiple_of(j * tk, tk), tk), :]
        cms = []
        for mp in range(2):
            s = _dot(kt, qz_ref[0, 0, mp])
            s_scr[slot, mp] = s
            cms.append(jnp.max(s, axis=0, keepdims=True))
        return tuple(cms)

    def probs(s_slot, p_slot, cms, ms):
        m_new, alphas = [], []
        for mp in range(2):
            m = jnp.maximum(ms[mp], cms[mp])
            alphas.append(jnp.exp2(ms[mp] - m))
            p_scr[p_slot, mp] = jnp.exp2(s_scr[s_slot, mp] - m).astype(bf16)
            m_new.append(m)
        return tuple(m_new), tuple(alphas)

    def accumulate(j, slot, alphas):
        vt = vt_ref[0, 0, j]
        for mp in range(2):
            acc_ref[mp] = acc_ref[mp] * alphas[mp] + _dot(vt, p_scr[slot, mp])

    lag = p_scr.shape[0]
    unroll = ATTN_UNROLL
    assert lag >= 2 and unroll % 2 == 0 and unroll % lag == 0 and nchunk >= lag

    def tick(t, r, ms, cm_prev, als):
        peeled = isinstance(t, int)
        cm = cm_prev
        if not peeled or t < nchunk:
            cm = scores(t, r % 2)
        if not peeled or 1 <= t <= nchunk:
            ms, al_new = probs((r - 1) % 2, (r - 1) % lag, cm_prev, ms)
            als = als + (al_new,)
        if not peeled or t >= lag:
            accumulate(t - lag, (r - lag) % lag, als[0])
            als = als[1:]
        return ms, cm, als

    first = lag + (nchunk - lag) % unroll
    ms = (jnp.full((1, tq), NEG_BIG, f32),) * 2
    cm, als = None, ()
    for t in range(first):
        ms, cm, als = tick(t, t % unroll, ms, cm, als)

    def ticks(it, carry):
        ms, cm, als = carry
        t0 = first + it * unroll
        for u in range(unroll):
            ms, cm, als = tick(t0 + u, (first + u) % unroll, ms, cm, als)
        return ms, cm, als

    ms, cm, als = lax.fori_loop(0, (nchunk - first) // unroll, ticks, (ms, cm, als))
    for t in range(nchunk, nchunk + lag):
        ms, cm, als = tick(t, t % unroll, ms, cm, als)

    lp = lam_ref[...]
    lam = (jnp.exp(jnp.sum(lp[0:1] * lp[1:2], axis=1, keepdims=True))
           - jnp.exp(jnp.sum(lp[2:3] * lp[3:4], axis=1, keepdims=True)) + LAMBDA_INIT)
    a0 = acc_ref[0]
    a1 = acc_ref[1]
    o = (a0[:DIFF_VDIM] / a0[DIFF_VDIM:DIFF_VDIM + 1]
         - lam * (a1[:DIFF_VDIM] / a1[DIFF_VDIM:DIFF_VDIM + 1]))
    ms2 = jnp.mean(jnp.square(o), axis=0, keepdims=True)
    y = o * lax.rsqrt(ms2 + SUBLN_EPS) * g_ref[...] * (1.0 - LAMBDA_INIT)
    o_ref[0] = y.T.astype(bf16)


def _attention(qz, kk, vt5, lam_params, g_col):
    B, H, _, _, S = qz.shape
    nchunk, tk = vt5.shape[2], vt5.shape[4]
    return pl.pallas_call(
        _attn_kernel,
        grid=(B, H, S // TQ),
        in_specs=[
            pl.BlockSpec((1, 1, 2, LANES, TQ), lambda b, h, i: (b, h, 0, 0, i)),
            pl.BlockSpec((1, 1, S, LANES), lambda b, h, i: (b, h, 0, 0)),
            pl.BlockSpec((1, 1, nchunk, V_ROWS, tk), lambda b, h, i: (b, h, 0, 0, 0)),
            pl.BlockSpec(lam_params.shape, lambda b, h, i: (0, 0)),
            pl.BlockSpec(g_col.shape, lambda b, h, i: (0, 0)),
        ],
        out_specs=pl.BlockSpec((1, TQ, DIFF_VDIM), lambda b, h, i: (b, i, h)),
        out_shape=jax.ShapeDtypeStruct((B, S, V_WIDTH), bf16),
        scratch_shapes=[pltpu.VMEM((2, 2, tk, TQ), f32),
                        pltpu.VMEM((ATTN_PV_LAG, 2, tk, TQ), bf16),
                        pltpu.VMEM((2, V_ROWS, TQ), f32)],
        compiler_params=_cparams(("parallel", "parallel", "parallel")),
        name="diffattn",
    )(qz, kk, vt5, lam_params, g_col)


def _fnet1_kernel(w1_ref, f_ref, a_ref):
    a_ref[0] = _dot(w1_ref[...], f_ref[0]).astype(bf16)


def _fnet1(f2d, w1):
    B, n1, cols = f2d.shape
    return pl.pallas_call(
        _fnet1_kernel,
        grid=(B, cols // F1_COLS),
        in_specs=[pl.BlockSpec(w1.shape, lambda b, j: (0, 0)),
                  pl.BlockSpec((1, n1, F1_COLS), lambda b, j: (b, 0, j))],
        out_specs=pl.BlockSpec((1, 2 * n1, F1_COLS), lambda b, j: (b, 0, j)),
        out_shape=jax.ShapeDtypeStruct((B, 2 * n1, cols), bf16),
        compiler_params=_cparams(("parallel", "parallel")),
        name="fnet_stage1",
    )(w1, f2d)


def _fnet2_kernel(m_ref, a_ref, cc_ref, sc_ref, wf_ref, o_ref):
    n2 = a_ref.shape[3]
    nk = a_ref.shape[2]
    groups = [slice(g * FNET_CH, (g + 1) * FNET_CH) for g in range(FNET_GROUPS)]
    bris = [_dot(m_ref[kk], jnp.concatenate([a_ref[0, 0, kk], a_ref[0, 1, kk]], axis=0))
            for kk in range(nk)]
    brs = [b[:n2].astype(bf16) for b in bris]
    bis = [b[n2:].astype(bf16) for b in bris]
    frs = [[(_dot(brs[kk][:, sl], cc_ref[...]) + _dot(bis[kk][:, sl], sc_ref[...])).astype(bf16) for sl in groups]
           for kk in range(nk)]
    for kk in range(nk):
        for g, sl in enumerate(groups):
            fo = _dot(frs[kk][g], wf_ref[g])
            o_ref[0, :, kk * F_WIDTH + g * FNET_CH:kk * F_WIDTH + (g + 1) * FNET_CH] = fo.astype(bf16)


def _fnet2(a5, mtab, cc, sc, wfn):
    B, _, n1, n2, c = a5.shape
    return pl.pallas_call(
        _fnet2_kernel,
        grid=(B, n1 // F2_K1),
        in_specs=[pl.BlockSpec((F2_K1, 2 * n2, 2 * n2), lambda b, j: (j, 0, 0)),
                  pl.BlockSpec((1, 2, F2_K1, n2, c), lambda b, j: (b, 0, j, 0, 0)),
                  pl.BlockSpec(cc.shape, lambda b, j: (0, 0)),
                  pl.BlockSpec(sc.shape, lambda b, j: (0, 0)),
                  pl.BlockSpec(wfn.shape, lambda b, j: (0, 0, 0))],
        out_specs=pl.BlockSpec((1, n2, F2_K1 * c), lambda b, j: (b, 0, j)),
        out_shape=jax.ShapeDtypeStruct((B, n2, n1 * c), bf16),
        compiler_params=_cparams(("parallel", "parallel")),
        name="fnet_stage2",
    )(mtab, a5, cc, sc, wfn)


def _dft_tables(S):
    n1, n2 = S // FN2, FN2
    i1 = jnp.arange(n1, dtype=jnp.int32)
    ang1 = (2.0 * math.pi / n1) * ((i1[:, None] * i1[None, :]) % n1).astype(f32)
    w1 = jnp.concatenate([jnp.cos(ang1), -jnp.sin(ang1)], axis=0) * (n1 ** -0.5)
    k = i1[:, None, None] + n1 * jnp.arange(n2, dtype=jnp.int32)[None, :, None]
    nn = jnp.arange(n2, dtype=jnp.int32)[None, None, :]
    ang2 = (2.0 * math.pi / S) * ((k * nn) % S).astype(f32)
    c2 = jnp.cos(ang2) * (n2 ** -0.5)
    s2 = jnp.sin(ang2) * (n2 ** -0.5)
    mtab = jnp.concatenate([jnp.concatenate([c2, s2], axis=2),
                            jnp.concatenate([-s2, c2], axis=2)], axis=1)
    ic = jnp.arange(FNET_CH, dtype=jnp.int32)
    angc = (2.0 * math.pi / FNET_CH) * ((ic[:, None] * ic[None, :]) % FNET_CH).astype(f32)
    cc = jnp.cos(angc) * (FNET_CH ** -0.5)
    sc = jnp.sin(angc) * (FNET_CH ** -0.5)
    return w1.astype(bf16), mtab.astype(bf16), cc.astype(bf16), sc.astype(bf16)


def _memkv_kernel(mem_ref, w_ref, k_ref, v_ref):
    kv = _dot(mem_ref[0].astype(bf16), w_ref[...])
    k_ref[0] = kv[:, :D_MODEL].astype(bf16)
    v_ref[0] = kv[:, D_MODEL:].astype(bf16)


def _memkv(mem, w_mkv):
    B, M, D = mem.shape
    return pl.pallas_call(
        _memkv_kernel,
        grid=(B,),
        in_specs=[pl.BlockSpec((1, M, D), lambda b: (b, 0, 0)),
                  pl.BlockSpec(w_mkv.shape, lambda b: (0, 0))],
        out_specs=[pl.BlockSpec((1, M, D), lambda b: (b, 0, 0))] * 2,
        out_shape=[jax.ShapeDtypeStruct((B, M, D), bf16)] * 2,
        compiler_params=_cparams(("parallel",)),
        name="memkv",
    )(mem, w_mkv)


def _post_kernel(x_ref, o_ref, fo_ref, wo_ref, g1_ref, b1_ref, wmq_ref, km_ref, vm_ref, wmo_ref,
                 g2_ref, b2_ref, wrh_ref, wrl_ref, br_ref,
                 x2_ref, x2t_ref, idx_ref, gate_ref, rank_ref, cnt_ref, carry_ref):
    first = jnp.logical_and(pl.program_id(0) == 0, pl.program_id(1) == 0)

    @pl.when(first)
    def _():
        carry_ref[...] = jnp.zeros_like(carry_ref)

    n_sub = x_ref.shape[1] // POST_SUB
    gens = [_post_rows(slice(s * POST_SUB, (s + 1) * POST_SUB), x_ref, o_ref, fo_ref, wo_ref, g1_ref, b1_ref,
                       wmq_ref, km_ref, vm_ref, wmo_ref, g2_ref, b2_ref, wrh_ref, wrl_ref, br_ref, x2_ref, x2t_ref)
            for s in range(n_sub)]
    routed = [None] * n_sub
    active = list(range(n_sub))
    while active:
        for s in list(active):
            try:
                next(gens[s])
            except StopIteration as done:
                routed[s] = done.value
                active.remove(s)

    carry = carry_ref[...]
    for s in range(n_sub):
        rows = slice(s * POST_SUB, (s + 1) * POST_SUB)
        lane, idxs, gates, before, sel_sum = routed[s]
        idx_out = jnp.zeros((POST_SUB, LANES), f32)
        gate_out = jnp.zeros((POST_SUB, LANES), f32)
        rank_out = jnp.zeros((POST_SUB, LANES), f32)
        for k in range(TOP_K):
            rk = jnp.sum(jnp.where(lane == idxs[k], before + carry, 0.0), axis=-1, keepdims=True)
            idx_out = jnp.where(lane == float(k), idxs[k], idx_out)
            gate_out = jnp.where(lane == float(k), gates[k], gate_out)
            rank_out = jnp.where(lane == float(k), rk, rank_out)
        idx_ref[0, rows] = idx_out.astype(jnp.int32)
        gate_ref[0, rows] = gate_out
        rank_ref[0, rows] = rank_out.astype(jnp.int32)
        carry = carry + sel_sum
    carry_ref[...] = carry
    cnt_ref[...] = carry.astype(jnp.int32)


def _post_rows(rows, x_ref, o_ref, fo_ref, wo_ref, g1_ref, b1_ref, wmq_ref, km_ref, vm_ref, wmo_ref,
               g2_ref, b2_ref, wrh_ref, wrl_ref, br_ref, x2_ref, x2t_ref):
    x = x_ref[0, rows]
    tt = x.shape[0]
    h = _dot(o_ref[0, rows], wo_ref[:V_WIDTH, :]) + _dot(fo_ref[0, rows], wo_ref[V_WIDTH:, :])
    yield
    x1 = _layer_norm(DEEPNORM_ALPHA * x + h, g1_ref[...], b1_ref[...])
    qm = _dot(x1.astype(bf16), wmq_ref[...]).astype(bf16)
    yield
    heads = []
    for hh in range(MEM_HEADS):
        sl = slice(hh * MEM_DH, (hh + 1) * MEM_DH)
        s = _dot_nt(qm[:, sl], km_ref[0, :, sl]) * (MEM_DH ** -0.5)
        s = s - jnp.max(s, axis=-1, keepdims=True)
        e = jnp.exp(s)
        p = e / jnp.sum(e, axis=-1, keepdims=True)
        heads.append(_dot(p.astype(bf16), vm_ref[0, :, sl]))
        yield
    om = jnp.concatenate(heads, axis=-1).astype(bf16)
    h2 = _dot(om, wmo_ref[...])
    yield
    x2 = _layer_norm(DEEPNORM_ALPHA * x1 + h2, g2_ref[...], b2_ref[...])
    x2_ref[0, rows] = x2
    _store_token_tiles(x2t_ref, (0,), rows.start, x2)

    xh = x2.astype(bf16)
    xl = (x2 - xh.astype(f32)).astype(bf16)
    logits = (_dot(xh, wrh_ref[...]) + _dot(xl, wrh_ref[...]) + _dot(xh, wrl_ref[...])) + br_ref[...]
    yield

    lane = lax.broadcasted_iota(jnp.int32, (tt, LANES), 1).astype(f32)
    work = logits
    vals, idxs = [], []
    for _ in range(TOP_K):
        mx = jnp.max(work, axis=-1, keepdims=True)
        ix = jnp.min(jnp.where(work == mx, lane, float(LANES)), axis=-1, keepdims=True)
        vals.append(mx)
        idxs.append(ix)
        work = jnp.where(lane == ix, -jnp.inf, work)
    es = [jnp.exp(v - vals[0]) for v in vals]
    den = es[0] + es[1] + es[2] + es[3]
    gates = [e / den for e in es]
    yield

    sel = jnp.zeros((tt, LANES), f32)
    for ix in idxs:
        sel = sel + jnp.where(lane == ix, 1.0, 0.0)
    row = lax.broadcasted_iota(jnp.int32, (tt, tt), 0)
    col = lax.broadcasted_iota(jnp.int32, (tt, tt), 1)
    ltri = jnp.where(col < row, 1.0, 0.0).astype(bf16)
    before = _dot(ltri, sel.astype(bf16))
    return lane, idxs, gates, before, jnp.sum(sel, axis=0, keepdims=True)


def _post(x, o, fo, wo, g1, b1, wmq, km, vm, wmo, g2, b2, wrh, wrl, br):
    B, S, D = x.shape
    tok = lambda w: pl.BlockSpec((1, POST_TT, w), lambda b, i: (b, i, 0))
    const = lambda a: pl.BlockSpec(a.shape, lambda b, i: (0,) * a.ndim)
    mem = lambda a: pl.BlockSpec((1,) + a.shape[1:], lambda b, i: (b, 0, 0))
    return pl.pallas_call(
        _post_kernel,
        grid=(B, S // POST_TT),
        in_specs=[tok(D), tok(V_WIDTH), tok(F_WIDTH), const(wo), const(g1), const(b1), const(wmq),
                  mem(km), mem(vm), const(wmo), const(g2), const(b2), const(wrh), const(wrl), const(br)],
        out_specs=[tok(D), pl.BlockSpec((1, POST_TT * SUBLANES, LANES), lambda b, i: (b, i, 0)),
                   tok(LANES), tok(LANES), tok(LANES),
                   pl.BlockSpec((1, LANES), lambda b, i: (0, 0))],
        out_shape=[jax.ShapeDtypeStruct((B, S, D), f32),
                   jax.ShapeDtypeStruct((B, S * SUBLANES, LANES), f32),
                   jax.ShapeDtypeStruct((B, S, LANES), jnp.int32),
                   jax.ShapeDtypeStruct((B, S, LANES), f32),
                   jax.ShapeDtypeStruct((B, S, LANES), jnp.int32),
                   jax.ShapeDtypeStruct((1, LANES), jnp.int32)],
        scratch_shapes=[pltpu.VMEM((1, LANES), f32)],
        compiler_params=_cparams(("arbitrary", "arbitrary")),
        name="post_mixer_router",
    )(x, o, fo, wo, g1, b1, wmq, km, vm, wmo, g2, b2, wrh, wrl, br)


def _tile_rows(t):
    return pl.ds(pl.multiple_of(t * SUBLANES, SUBLANES), SUBLANES)


def _row_copy_out(x_ref, xs_ref, sem, t, d):
    return pltpu.make_async_copy(x_ref.at[_tile_rows(t)], xs_ref.at[_tile_rows(d)], sem)


def _zero_copy(zero_ref, xs_ref, sem, row):
    return pltpu.make_async_copy(
        zero_ref, xs_ref.at[pl.ds(pl.multiple_of(row * SUBLANES, SUBLANES), zero_ref.shape[0])], sem)


def _dispatch_kernel(zrow_ref, zvalid_ref, dest_ref, x_ref, xs_ref, zero_ref, zsem, sem):
    i = pl.program_id(0)
    rows_per_tile = TT * TOP_K

    @pl.when(i == 0)
    def _():
        zero_ref[...] = jnp.zeros_like(zero_ref)
        for j in range(zrow_ref.shape[0]):
            @pl.when(zvalid_ref[j] == 1)
            def _():
                _zero_copy(zero_ref, xs_ref, zsem, zrow_ref[j]).start()
        for j in range(zrow_ref.shape[0]):
            @pl.when(zvalid_ref[j] == 1)
            def _():
                _zero_copy(zero_ref, xs_ref, zsem, 0).wait()

    def issue(g, c):
        r0 = g * DMA_UNROLL
        dests = [dest_ref[0, 0, r0 * TOP_K + j] for j in range(DMA_UNROLL * TOP_K)]
        for u in range(DMA_UNROLL):
            for k in range(TOP_K):
                _row_copy_out(x_ref, xs_ref, sem, r0 + u, dests[u * TOP_K + k]).start(priority=k % 2)
        return c

    lax.fori_loop(0, TT // DMA_UNROLL, issue, 0)

    def drain(g, c):
        for u in range(DMA_UNROLL * TOP_K):
            _row_copy_out(x_ref, xs_ref, sem, 0, 0).wait()
        return c

    lax.fori_loop(0, rows_per_tile // (DMA_UNROLL * TOP_K), drain, 0)


def _dispatch(zrows, zvalid, dest3, x2_tiles, R):
    T = x2_tiles.shape[0] // SUBLANES
    grid_spec = pltpu.PrefetchScalarGridSpec(
        num_scalar_prefetch=2,
        grid=(T // TT,),
        in_specs=[pl.BlockSpec((1, 1, TT * TOP_K), lambda i, z, zv: (i, 0, 0), memory_space=pltpu.SMEM),
                  pl.BlockSpec((TT * SUBLANES, LANES), lambda i, z, zv: (i, 0))],
        out_specs=pl.BlockSpec(memory_space=pl.ANY),
        scratch_shapes=[pltpu.VMEM((EXPERT_BM * SUBLANES, LANES), f32),
                        pltpu.SemaphoreType.DMA, pltpu.SemaphoreType.DMA],
    )
    return pl.pallas_call(
        _dispatch_kernel,
        grid_spec=grid_spec,
        out_shape=jax.ShapeDtypeStruct((R * SUBLANES, LANES), f32),
        compiler_params=_cparams(("arbitrary",)),
        name="moe_dispatch",
    )(zrows, zvalid, dest3, x2_tiles)


def _expert_kernel(be_ref, nused_ref, xs_ref, wgu_ref, bgu_ref, wd_ref, bd_ref, ys_ref):
    del be_ref
    in_use = pl.program_id(0) < nused_ref[0]

    @pl.when(in_use)
    def _():
        _expert_block(xs_ref, wgu_ref, bgu_ref, wd_ref, bd_ref, ys_ref)

    @pl.when(jnp.logical_not(in_use))
    def _():
        ys_ref[...] = jnp.zeros_like(ys_ref)


def _expert_block(xs_ref, wgu_ref, bgu_ref, wd_ref, bd_ref, ys_ref):
    bm = xs_ref.shape[0] // SUBLANES
    n_sub = bm // EXPERT_SUB
    xs = [jnp.concatenate([xs_ref[pl.ds(s * EXPERT_SUB * SUBLANES + j, EXPERT_SUB, stride=SUBLANES)]
                           for j in range(D_MODEL // LANES)], axis=-1).astype(bf16) for s in range(n_sub)]
    hbs = [_dot(x, wgu_ref[0]) + bgu_ref[0] for x in xs]
    acts = []
    for hb in hbs:
        g = jnp.minimum(hb[:, :D_FF], SWIGLU_LIMIT)
        u = jnp.clip(hb[:, D_FF:], -SWIGLU_LIMIT, SWIGLU_LIMIT)
        acts.append(((u + 1.0) * (g * jax.nn.sigmoid(g * SWIGLU_ALPHA))).astype(bf16))
    for s, a in enumerate(acts):
        _store_token_tiles(ys_ref, (), s * EXPERT_SUB, _dot(a, wd_ref[0]) + bd_ref[0])


def _experts(block_e, n_used, xs, wgu, bgu, wd, bd):
    D = D_MODEL
    rows = EXPERT_BM * SUBLANES
    grid_spec = pltpu.PrefetchScalarGridSpec(
        num_scalar_prefetch=2,
        grid=(xs.shape[0] // rows,),
        in_specs=[pl.BlockSpec((rows, LANES), lambda i, be, nu: (i, 0)),
                  pl.BlockSpec((1, D, 2 * D_FF), lambda i, be, nu: (be[i], 0, 0)),
                  pl.BlockSpec((1, 1, 2 * D_FF), lambda i, be, nu: (be[i], 0, 0)),
                  pl.BlockSpec((1, D_FF, D), lambda i, be, nu: (be[i], 0, 0)),
                  pl.BlockSpec((1, 1, D), lambda i, be, nu: (be[i], 0, 0))],
        out_specs=pl.BlockSpec((rows, LANES), lambda i, be, nu: (i, 0)),
    )
    return pl.pallas_call(
        _expert_kernel,
        grid_spec=grid_spec,
        out_shape=jax.ShapeDtypeStruct(xs.shape, f32),
        compiler_params=_cparams(("arbitrary",)),
        name="moe_experts",
    )(block_e, n_used, xs, wgu, bgu, wd, bd)


def _row_copy_in(ys_ref, buf_ref, sems, slot, k, r, d):
    return pltpu.make_async_copy(ys_ref.at[_tile_rows(d)], buf_ref.at[slot, k, _tile_rows(r)], sems.at[slot])


def _combine_kernel(dest_ref, dest_next_ref, ys_ref, x2_ref, gate_ref, g3_ref, b3_ref, out_ref, buf_ref, sems):
    i = pl.program_id(0)
    slot = i % 2

    def start_tile(d_ref, s):
        def issue(g, c):
            r0 = g * DMA_UNROLL
            srcs = [d_ref[0, 0, r0 * TOP_K + j] for j in range(DMA_UNROLL * TOP_K)]
            for u in range(DMA_UNROLL):
                for k in range(TOP_K):
                    _row_copy_in(ys_ref, buf_ref, sems, s, k, r0 + u, srcs[u * TOP_K + k]).start(priority=k % 2)
            return c

        lax.fori_loop(0, TT // DMA_UNROLL, issue, 0)

    @pl.when(i == 0)
    def _():
        start_tile(dest_ref, 0)

    @pl.when(i + 1 < pl.num_programs(0))
    def _():
        start_tile(dest_next_ref, 1 - slot)

    def drain(g, c):
        for u in range(DMA_UNROLL * TOP_K):
            _row_copy_in(ys_ref, buf_ref, sems, slot, 0, 0, 0).wait()
        return c

    lax.fori_loop(0, TT // DMA_UNROLL, drain, 0)

    gates = gate_ref[...]
    y = gates[:, 0:1] * _load_token_tiles(buf_ref, (slot, 0), TT)
    for k in range(1, TOP_K):
        y = y + gates[:, k:k + 1] * _load_token_tiles(buf_ref, (slot, k), TT)
    out_ref[...] = _layer_norm(DEEPNORM_ALPHA * x2_ref[...] + y, g3_ref[...], b3_ref[...])


def _combine(dest3, ys, x2_flat, gates_flat, g3, b3):
    T, D = x2_flat.shape
    n_tiles = T // TT
    dest_spec = lambda off: pl.BlockSpec((1, 1, TT * TOP_K), lambda i: (jnp.minimum(i + off, n_tiles - 1), 0, 0),
                                         memory_space=pltpu.SMEM)
    return pl.pallas_call(
        _combine_kernel,
        grid=(n_tiles,),
        in_specs=[dest_spec(0), dest_spec(1),
                  pl.BlockSpec(memory_space=pl.ANY),
                  pl.BlockSpec((TT, D), lambda i: (i, 0)),
                  pl.BlockSpec((TT, LANES), lambda i: (i, 0)),
                  pl.BlockSpec(g3.shape, lambda i: (0, 0)),
                  pl.BlockSpec(b3.shape, lambda i: (0, 0))],
        out_specs=pl.BlockSpec((TT, D), lambda i: (i, 0)),
        out_shape=jax.ShapeDtypeStruct((T, D), f32),
        scratch_shapes=[pltpu.VMEM((2, TOP_K, TT * SUBLANES, LANES), f32), pltpu.SemaphoreType.DMA((2,))],
        compiler_params=_cparams(("arbitrary",)),
        name="moe_combine",
    )(dest3, dest3, ys, x2_flat, gates_flat, g3, b3)


def _rope_tables(S):
    half = DIFF_DH // 2
    inv = 1.0 / (ROPE_THETA ** (jnp.arange(half, dtype=f32) / half))
    ang = jnp.arange(S, dtype=f32)[:, None] * inv[None, :]
    cos = jnp.concatenate([jnp.cos(ang), jnp.cos(ang)], -1)
    sin = jnp.concatenate([jnp.sin(ang), jnp.sin(ang)], -1)
    return (jnp.concatenate([cos, cos], -1), jnp.concatenate([sin, sin], -1), cos.T, sin.T)


def _prep_weights(w_in, w_fnet, w_o, w_mq, w_mkv, w_mo, w_router, b_router, w_gu, w_down):
    wq = w_in[:, :QK_WIDTH]
    wk = w_in[:, QK_WIDTH:2 * QK_WIDTH]
    wv = w_in[:, 2 * QK_WIDTH:2 * QK_WIDTH + V_WIDTH]
    wf = w_in[:, 2 * QK_WIDTH + V_WIDTH:]
    wqt = wq.T.astype(bf16)
    wkk = wk.astype(bf16)
    wvt = wv.T.astype(bf16)
    pad = LANES - N_EXPERTS
    wr = jnp.pad(w_router, ((0, 0), (0, pad)))
    wrh = wr.astype(bf16)
    wrl = (wr - wrh.astype(f32)).astype(bf16)
    br = jnp.pad(b_router, (0, pad), constant_values=NEG_BIG).reshape(1, LANES)
    return dict(wqt=wqt, wk=wkk, wvt=wvt, wf=wf.astype(bf16), wfn=w_fnet.astype(bf16),
                wo=w_o.astype(bf16), wmq=w_mq.astype(bf16), wmkv=w_mkv.astype(bf16),
                wmo=w_mo.astype(bf16), wrh=wrh, wrl=wrl, br=br,
                wgu=w_gu.astype(bf16), wd=w_down.astype(bf16))


def _trunk(x, mem, W, tables, lam_params, g_col, ln, b_gu, b_down):
    B, S, D = x.shape
    cosk, sink, cost, sint = tables["rope"]
    qz, kk, vt5, f = _inproj(x, W["wqt"], W["wk"], W["wvt"], W["wf"], cosk, sink, cost, sint)
    o = _attention(qz, kk, vt5, lam_params, g_col)

    n1 = S // FN2
    w1, mtab, cc, sc = tables["dft"]
    a = _fnet1(f.reshape(B, n1, FN2 * F_WIDTH), w1)
    fo = _fnet2(a.reshape(B, 2, n1, FN2, F_WIDTH), mtab, cc, sc, W["wfn"]).reshape(B, S, F_WIDTH)

    km, vm = _memkv(mem, W["wmkv"])
    x2, x2t, idx, gates, rank, counts = _post(x, o, fo, W["wo"], ln["g1"], ln["b1"], W["wmq"], km, vm,
                                         W["wmo"], ln["g2"], ln["b2"], W["wrh"], W["wrl"], W["br"])

    T = B * S
    N = T * TOP_K
    bm = EXPERT_BM
    counts = counts[0, :N_EXPERTS]
    padded = (counts + bm - 1) // bm * bm
    pstarts = jnp.cumsum(padded) - padded
    R = N + N_EXPERTS * bm
    nb = R // bm
    ends = jnp.cumsum(padded)
    block_e = jnp.minimum(
        jnp.sum((ends[None, :] <= (jnp.arange(nb, dtype=jnp.int32) * bm)[:, None]).astype(jnp.int32), axis=1),
        N_EXPERTS - 1).astype(jnp.int32)
    top_i = idx.reshape(T, LANES)[:, :TOP_K]
    onehot = top_i[:, :, None] == jnp.arange(N_EXPERTS, dtype=jnp.int32)[None, None, :]
    dest = jnp.sum(jnp.where(onehot, pstarts[None, None, :], 0), axis=-1) + rank.reshape(T, LANES)[:, :TOP_K]
    dest3 = dest.astype(jnp.int32).reshape(T // TT, 1, TT * TOP_K)

    last_blk = (nb - 1) * bm
    seg_valid = padded > 0
    seg_last = jnp.where(seg_valid, pstarts + padded - bm, 0)
    tail_rows = ends[-1] + jnp.arange(N_EXPERTS, dtype=jnp.int32) * bm
    tail_valid = tail_rows <= last_blk
    zrows = jnp.concatenate([seg_last, jnp.where(tail_valid, tail_rows, 0)]).astype(jnp.int32)
    zvalid = jnp.concatenate([seg_valid, tail_valid]).astype(jnp.int32)

    x2f = x2.reshape(T, D)
    xs = _dispatch(zrows, zvalid, dest3, x2t.reshape(T * SUBLANES, LANES), R)
    n_used = (ends[-1] // bm).astype(jnp.int32).reshape(1)
    ys = _experts(block_e, n_used, xs, W["wgu"], b_gu, W["wd"], b_down)
    out = _combine(dest3, ys, x2f, gates.reshape(T, LANES), ln["g3"], ln["b3"])
    return out.reshape(B, S, D)


def kernel(x_prompt, x_sample, mem_prompt, mem_sample, w_in, lambda_q1, lambda_k1, lambda_q2, lambda_k2,
           subln_g, w_fnet, w_o, ln1_g, ln1_b, w_mq, w_mkv, w_mo, ln2_g, ln2_b, w_router, b_router,
           w_gu, b_gu, w_down, b_down, ln3_g, ln3_b):
    l = 0
    W = _prep_weights(w_in[l], w_fnet[l], w_o[l], w_mq[l], w_mkv[l], w_mo[l], w_router[l], b_router[l],
                      w_gu[l], w_down[l])
    lam_params = jnp.stack([lambda_q1[l], lambda_k1[l], lambda_q2[l], lambda_k2[l]], axis=0)
    g_col = subln_g[l].reshape(DIFF_VDIM, 1)
    row = lambda v: v[l].reshape(1, -1)
    ln = dict(g1=row(ln1_g), b1=row(ln1_b), g2=row(ln2_g), b2=row(ln2_b), g3=row(ln3_g), b3=row(ln3_b))
    bgu = b_gu[l].reshape(N_EXPERTS, 1, 2 * D_FF)
    bdn = b_down[l].reshape(N_EXPERTS, 1, D_MODEL)
    tables = {}
    outs = []
    for x, mem in ((x_prompt, mem_prompt), (x_sample, mem_sample)):
        S = x.shape[1]
        if S not in tables:
            tables[S] = dict(rope=_rope_tables(S), dft=_dft_tables(S))
        outs.append(_trunk(x, mem, W, tables[S], lam_params, g_col, ln, bgu, bdn))
    return tuple(outs)
```
